```python
import jax, jax.numpy as jnp
from jax import lax
import numpy as np

D_MODEL = 1024
BATCH = 8
SEQ = 2048
DEPTH = 2

N_GROUPS = 4
GROUP_HEADS = 4
GROUP_WIDTH = D_MODEL // N_GROUPS
HEAD_DIM = GROUP_WIDTH // GROUP_HEADS
MIX_WIDTH = N_GROUPS * GROUP_WIDTH
CONV_WIDTH = 3
RWKV_DECAY_RANK = 64
RWKV_ICLR_RANK = 64
RWKV_GATE_RANK = 128
RWKV_GN_EPS = 64e-5
RET_CHUNK = 128
MLA_Q_RANK = 384
MLA_KV_RANK = 128
MLA_NOPE = 64
MLA_ROPE = 32
MLA_V = 64
ATTN_BLOCK = 128
D_FF = 2816
ROPE_BASE = 10000.0
NORM_EPS = 1e-6
MAX_POS_OFFSET = 4096
IN_SPLITS = (
    GROUP_WIDTH, GROUP_WIDTH, GROUP_WIDTH,
    GROUP_WIDTH, GROUP_WIDTH, GROUP_WIDTH,
    RWKV_DECAY_RANK, RWKV_DECAY_RANK,
    RWKV_ICLR_RANK, RWKV_ICLR_RANK,
    RWKV_GATE_RANK,
    GROUP_WIDTH, GROUP_WIDTH, GROUP_WIDTH, GROUP_WIDTH,
    MLA_Q_RANK, MLA_KV_RANK + MLA_ROPE,
)
IN_WIDTH = sum(IN_SPLITS)

kernel_name = 'hybrid_parallel_mixer_encoder'


def rmsnorm(x, g):
    xf = x.astype(jnp.float32)
    y = xf * lax.rsqrt(jnp.mean(xf * xf, axis=-1, keepdims=True) + NORM_EPS)
    return (y * g.astype(jnp.float32)).astype(x.dtype)


def swiglu(x, w_gate, w_up, w_down):
    return (jax.nn.silu(x @ w_gate) * (x @ w_up)) @ w_down


def rope(x, pos):
    half = x.shape[-1] // 2
    inv = ROPE_BASE ** (-jnp.arange(half, dtype=jnp.float32) / half)
    ang = pos.astype(jnp.float32)[:, :, None, None] * inv
    cos, sin = jnp.cos(ang), jnp.sin(ang)
    xf = x.astype(jnp.float32)
    x1, x2 = xf[..., :half], xf[..., half:]
    return jnp.concatenate([x1 * cos - x2 * sin, x1 * sin + x2 * cos], axis=-1).astype(x.dtype)


def heads(t):
    b, s, _ = t.shape
    return t.reshape(b, s, GROUP_HEADS, HEAD_DIM).astype(jnp.float32)


def short_conv_mixer(x_in, b_gate, c_gate, conv_w):
    u = c_gate * x_in
    y = lax.conv_general_dilated(
        u, conv_w[:, None, :].astype(u.dtype), window_strides=(1,), padding=[(1, 1)],
        dimension_numbers=('NWC', 'WIO', 'NWC'), feature_group_count=GROUP_WIDTH)
    return b_gate * y


def rwkv7_scan(r, k, v, kk, a, decay, reverse):
    xs = tuple(jnp.moveaxis(t, 1, 0) for t in (r, k, v, kk, a, decay))
    b, _, h, n = r.shape

    def step(S, inp):
        r_t, k_t, v_t, kk_t, a_t, w_t = inp
        sa = jnp.einsum('bhvk,bhk->bhv', S, -kk_t)
        S = (S * w_t[:, :, None, :] + sa[..., None] * (kk_t * a_t)[:, :, None, :]
             + v_t[..., None] * k_t[:, :, None, :])
        return S, jnp.einsum('bhvk,bhk->bhv', S, r_t)

    S0 = jnp.zeros((b, h, n, n), jnp.float32)
    _, y = lax.scan(step, S0, xs, reverse=reverse)
    return jnp.moveaxis(y, 0, 1)


def rwkv7_mixer(r, k, v, wd_f, wd_b, ad_f, ad_b, gd, w0_f, w0_b, w2_f, w2_b,
                a0_f, a0_b, a2_f, a2_b, g2, k_k, k_a, r_k, lnx_g, lnx_b):
    b, s, _ = r.shape
    g = jax.nn.sigmoid(gd) @ g2
    kk = heads(k * k_k)
    kk = kk / jnp.maximum(jnp.sqrt(jnp.sum(kk * kk, axis=-1, keepdims=True)), 1e-12)
    rh, vh = heads(r), heads(v)
    y = jnp.zeros_like(rh)
    k_sum = jnp.zeros_like(rh)
    for wd, ad, w0, w2, a0, a2, rev in ((wd_f, ad_f, w0_f, w2_f, a0_f, a2_f, False),
                                        (wd_b, ad_b, w0_b, w2_b, a0_b, a2_b, True)):
        w_log = -jax.nn.softplus(-(w0 + jnp.tanh(wd) @ w2)) - 0.5
        decay = jnp.exp(-jnp.exp(w_log.astype(jnp.float32)))
        a = jax.nn.sigmoid(a0 + ad @ a2)
        k_dir = heads(k * (1.0 + (a - 1.0) * k_a))
        y = y + rwkv7_scan(rh, k_dir, vh, kk, heads(a), heads(decay), rev)
        k_sum = k_sum + k_dir
    mu = jnp.mean(y, axis=-1, keepdims=True)
    var = jnp.mean(jnp.square(y - mu), axis=-1, keepdims=True)
    yn = ((y - mu) * lax.rsqrt(var + RWKV_GN_EPS)).reshape(b, s, GROUP_WIDTH)
    yn = yn * lnx_g.astype(jnp.float32) + lnx_b.astype(jnp.float32)
    bonus = (jnp.sum(rh * k_sum * r_k.astype(jnp.float32), axis=-1, keepdims=True) * vh).reshape(b, s, GROUP_WIDTH)
    return (yn + bonus).astype(r.dtype) * g


def retention_mixer(q, k, v, g, pos, gn_g):
    b, s, _ = q.shape
    nc, C = s // RET_CHUNK, RET_CHUNK
    qh = rope(q.reshape(b, s, GROUP_HEADS, HEAD_DIM), pos)
    kh = rope(k.reshape(b, s, GROUP_HEADS, HEAD_DIM), pos) * (HEAD_DIM ** -0.5)
    vh = v.reshape(b, s, GROUP_HEADS, HEAD_DIM)
    chunk = lambda t: t.reshape(b, nc, C, GROUP_HEADS, HEAD_DIM).transpose(0, 3, 1, 2, 4).astype(jnp.float32)
    qc, kc, vc = chunk(qh), chunk(kh), chunk(vh)
    log_gamma = jnp.log(1.0 - 2.0 ** (-5.0 - jnp.arange(GROUP_HEADS, dtype=jnp.float32)))
    idx = jnp.arange(C, dtype=jnp.float32)
    intra_decay = jnp.exp(log_gamma[:, None, None] * jnp.abs(idx[:, None] - idx[None, :]))
    dec_start = jnp.exp(log_gamma[:, None] * idx)[None, :, None, :, None]
    dec_end = jnp.exp(log_gamma[:, None] * (C - idx))[None, :, None, :, None]
    chunk_decay = jnp.exp(log_gamma * C)[None, :, None, None]
    scores = jnp.einsum('bhnid,bhnjd->bhnij', qc, kc) * intra_decay[None, :, None]
    o = jnp.einsum('bhnij,bhnjd->bhnid', scores, vc)
    kv_f = jnp.einsum('bhnjd,bhnje->nbhde', kc * dec_end, vc)
    kv_b = jnp.einsum('bhnjd,bhnje->nbhde', kc * dec_start, vc)

    def step(R, summ):
        return R * chunk_decay + summ, R

    R0 = jnp.zeros((b, GROUP_HEADS, HEAD_DIM, HEAD_DIM), jnp.float32)
    _, R_f = lax.scan(step, R0, kv_f)
    _, R_b = lax.scan(step, R0, kv_b, reverse=True)
    o = (o + dec_start * jnp.einsum('bhnid,nbhde->bhnie', qc, R_f)
         + dec_end * jnp.einsum('bhnid,nbhde->bhnie', qc, R_b))
    o = o.transpose(0, 2, 3, 1, 4).reshape(b, s, GROUP_HEADS, HEAD_DIM)
    o = o * lax.rsqrt(jnp.mean(o * o, axis=-1, keepdims=True) + NORM_EPS)
    o = o.reshape(b, s, GROUP_WIDTH) * gn_g.astype(jnp.float32)
    return jax.nn.silu(g) * o.astype(q.dtype)


def mla_mixer(q_a, kv_a, pos, q_a_norm, q_b, kv_a_norm, kv_b):
    b, s, _ = q_a.shape
    q = (rmsnorm(q_a, q_a_norm) @ q_b).reshape(b, s, GROUP_HEADS, MLA_NOPE + MLA_ROPE)
    q_nope, q_rope = q[..., :MLA_NOPE], rope(q[..., MLA_NOPE:], pos)
    c_kv, k_rope = kv_a[..., :MLA_KV_RANK], kv_a[..., MLA_KV_RANK:]
    kv = (rmsnorm(c_kv, kv_a_norm) @ kv_b).reshape(b, s, GROUP_HEADS, MLA_NOPE + MLA_V)
    k_nope, v = kv[..., :MLA_NOPE], kv[..., MLA_NOPE:]
    k_rope = rope(k_rope[:, :, None, :], pos)[:, :, 0]
    scale = (MLA_NOPE + MLA_ROPE) ** -0.5
    nb = s // ATTN_BLOCK
    blocks = lambda t: t.reshape(b, nb, ATTN_BLOCK, GROUP_HEADS, t.shape[-1]).transpose(1, 0, 2, 3, 4)

    def attend(blk):
        qn, qr = blk
        sc = jnp.einsum('bqhd,bkhd->bhqk', qn, k_nope) + jnp.einsum('bqhr,bkr->bhqk', qr, k_rope)
        p = jax.nn.softmax(sc.astype(jnp.float32) * scale, axis=-1)
        return jnp.einsum('bhqk,bkhd->bqhd', p.astype(v.dtype), v)

    o = lax.map(attend, (blocks(q_nope), blocks(q_rope)))
    return o.transpose(1, 0, 2, 3, 4).reshape(b, s, GROUP_WIDTH)


def setup_inputs(seed: int = 0) -> dict:
    key = jax.random.key(seed)
    ks = iter(jax.random.split(key, 48))
    L, W, H, N = DEPTH, GROUP_WIDTH, GROUP_HEADS, HEAD_DIM
    f32 = jnp.float32

    def nrm(shape):
        return jax.random.normal(next(ks), shape, f32)

    def dense(shape, fan_in, scale=1.0):
        return nrm(shape) * (scale * fan_in ** -0.5)

    def gain(shape):
        return 1.0 + 0.02 * nrm(shape)

    decay_base = jnp.tile(jnp.linspace(-6.0, -1.0, N, dtype=f32), H)
    x = nrm((BATCH, SEQ, D_MODEL))
    offsets = jax.random.randint(next(ks), (BATCH, 1), 0, MAX_POS_OFFSET, dtype=jnp.int32)
    positions = offsets + jnp.arange(SEQ, dtype=jnp.int32)[None, :]
    return {
        'x': x,
        'positions': positions,
        'ffn1_norm': gain((L, D_MODEL)),
        'ffn1_w_gate': dense((L, D_MODEL, D_FF), D_MODEL),
        'ffn1_w_up': dense((L, D_MODEL, D_FF), D_MODEL),
        'ffn1_w_down': dense((L, D_FF, D_MODEL), D_FF),
        'mix_norm': gain((L, D_MODEL)),
        'w_in': dense((L, D_MODEL, IN_WIDTH), D_MODEL),
        'w_out': dense((L, MIX_WIDTH, D_MODEL), MIX_WIDTH),
        'conv_w': dense((L, CONV_WIDTH, W), CONV_WIDTH),
        'rwkv_w0_f': decay_base + 0.1 * nrm((L, W)),
        'rwkv_w0_b': decay_base + 0.1 * nrm((L, W)),
        'rwkv_w2_f': dense((L, RWKV_DECAY_RANK, W), RWKV_DECAY_RANK, 0.1),
        'rwkv_w2_b': dense((L, RWKV_DECAY_RANK, W), RWKV_DECAY_RANK, 0.1),
        'rwkv_a0_f': 0.1 * nrm((L, W)),
        'rwkv_a0_b': 0.1 * nrm((L, W)),
        'rwkv_a2_f': dense((L, RWKV_ICLR_RANK, W), RWKV_ICLR_RANK, 0.1),
        'rwkv_a2_b': dense((L, RWKV_ICLR_RANK, W), RWKV_ICLR_RANK, 0.1),
        'rwkv_g2': dense((L, RWKV_GATE_RANK, W), RWKV_GATE_RANK),
        'rwkv_k_k': 0.85 + 0.02 * nrm((L, W)),
        'rwkv_k_a': 1.0 + 0.02 * nrm((L, W)),
        'rwkv_r_k': 0.1 * nrm((L, H, N)),
        'rwkv_lnx_g': gain((L, W)),
        'rwkv_lnx_b': 0.01 * nrm((L, W)),
        'ret_gn_g': gain((L, W)),
        'mla_q_a_norm': gain((L, MLA_Q_RANK)),
        'mla_q_b': dense((L, MLA_Q_RANK, H * (MLA_NOPE + MLA_ROPE)), MLA_Q_RANK),
        'mla_kv_a_norm': gain((L, MLA_KV_RANK)),
        'mla_kv_b': dense((L, MLA_KV_RANK, H * (MLA_NOPE + MLA_V)), MLA_KV_RANK),
        'ffn2_norm': gain((L, D_MODEL)),
        'ffn2_w_gate': dense((L, D_MODEL, D_FF), D_MODEL),
        'ffn2_w_up': dense((L, D_MODEL, D_FF), D_MODEL),
        'ffn2_w_down': dense((L, D_FF, D_MODEL), D_FF),
        'final_norm': gain((D_MODEL,)),
    }


def reference(x, positions, ffn1_norm, ffn1_w_gate, ffn1_w_up, ffn1_w_down, mix_norm, w_in, w_out,
              conv_w, rwkv_w0_f, rwkv_w0_b, rwkv_w2_f, rwkv_w2_b, rwkv_a0_f, rwkv_a0_b, rwkv_a2_f,
              rwkv_a2_b, rwkv_g2, rwkv_k_k, rwkv_k_a, rwkv_r_k, rwkv_lnx_g, rwkv_lnx_b, ret_gn_g,
              mla_q_a_norm, mla_q_b, mla_kv_a_norm, mla_kv_b, ffn2_norm, ffn2_w_gate, ffn2_w_up,
              ffn2_w_down, final_norm):
    split_points = tuple(int(p) for p in np.cumsum(IN_SPLITS)[:-1])
    for l in range(DEPTH):
        x = x + 0.5 * swiglu(rmsnorm(x, ffn1_norm[l]), ffn1_w_gate[l], ffn1_w_up[l], ffn1_w_down[l])
        h = rmsnorm(x, mix_norm[l])
        u = h @ w_in[l]
        (c_x, c_b, c_c, rw_r, rw_k, rw_v, rw_wd_f, rw_wd_b, rw_ad_f, rw_ad_b, rw_gd,
         rt_q, rt_k, rt_v, rt_g, ml_qa, ml_kva) = jnp.split(u, split_points, axis=-1)
        y_conv = short_conv_mixer(c_x, c_b, c_c, conv_w[l])
        y_rwkv = rwkv7_mixer(rw_r, rw_k, rw_v, rw_wd_f, rw_wd_b, rw_ad_f, rw_ad_b, rw_gd,
                             rwkv_w0_f[l], rwkv_w0_b[l], rwkv_w2_f[l], rwkv_w2_b[l],
                             rwkv_a0_f[l], rwkv_a0_b[l], rwkv_a2_f[l], rwkv_a2_b[l], rwkv_g2[l],
                             rwkv_k_k[l], rwkv_k_a[l], rwkv_r_k[l], rwkv_lnx_g[l], rwkv_lnx_b[l])
        y_ret = retention_mixer(rt_q, rt_k, rt_v, rt_g, positions, ret_gn_g[l])
        y_mla = mla_mixer(ml_qa, ml_kva, positions, mla_q_a_norm[l], mla_q_b[l], mla_kv_a_norm[l], mla_kv_b[l])
        x = x + jnp.concatenate([y_conv, y_rwkv, y_ret, y_mla], axis=-1) @ w_out[l]
        x = x + 0.5 * swiglu(rmsnorm(x, ffn2_norm[l]), ffn2_w_gate[l], ffn2_w_up[l], ffn2_w_down[l])
    return rmsnorm(x, final_norm)
```

```python
import functools
import math

import jax
import jax.numpy as jnp
from jax import lax
from jax.experimental import pallas as pl
from jax.experimental.pallas import tpu as pltpu

F32 = jnp.float32
BF16 = jnp.bfloat16

D_MODEL = 1024
GROUP_HEADS = 4
GROUP_WIDTH = 256
HEAD_DIM = 64
D_FF = 2816
RWKV_GN_EPS = 64e-5
MLA_Q_RANK = 384
MLA_KV_RANK = 128
MLA_NOPE = 64
MLA_ROPE = 32
ROPE_BASE = 10000.0
NORM_EPS = 1e-6
RET_LOG_GAMMA = tuple(math.log(1.0 - 2.0 ** (-5.0 - h)) for h in range(GROUP_HEADS))

CONV_COLS = (0, 768)
RWKV_COLS = (768, 1920)
RET_COLS = (1920, 2944)
MLA_COLS = (2944, 3488)
MLA_PAD = 640

VMEM_LIMIT = 56 * 1024 * 1024


def _cparams(*sem):
    return pltpu.CompilerParams(dimension_semantics=sem, vmem_limit_bytes=VMEM_LIMIT)


def _rms(x, g):
    return x * lax.rsqrt(jnp.mean(x * x, axis=-1, keepdims=True) + NORM_EPS) * g


def _head_ones(width, head):
    r = lax.broadcasted_iota(jnp.int32, (width, width), 0) // head
    c = lax.broadcasted_iota(jnp.int32, (width, width), 1) // head
    return r == c


def _head_sum(x, head=HEAD_DIM):
    g = _head_ones(x.shape[-1], head).astype(F32)
    return jnp.dot(x, g, precision=lax.Precision.HIGHEST, preferred_element_type=F32)


def _rotate_half(x, half):
    n = x.shape[-1]
    lane = lax.broadcasted_iota(jnp.int32, x.shape, x.ndim - 1)
    first = (lane % (2 * half)) < half
    from_right = pltpu.roll(x, n - half, x.ndim - 1)
    from_left = pltpu.roll(x, half, x.ndim - 1)
    return jnp.where(first, -from_right, from_left)


def _rope(x, pos, half):
    lane = lax.broadcasted_iota(jnp.int32, (1, x.shape[-1]), 1)
    inv = jnp.exp((lane % half).astype(F32) * (-math.log(ROPE_BASE) / half))
    ang = pos * inv
    return x * jnp.cos(ang) + _rotate_half(x, half) * jnp.sin(ang)


def _ffn_kernel(x_ref, g_ref, wg_ref, wu_ref, wd_ref, *rest, n_f, final):
    if final:
        fg_ref, o_ref, h_ref, acc_ref = rest
    else:
        o_ref, h_ref, acc_ref = rest
    j = pl.program_id(1)

    @pl.when(j == 0)
    def _():
        h_ref[...] = _rms(x_ref[...], g_ref[...]).astype(BF16)
        acc_ref[...] = jnp.zeros_like(acc_ref)

    h = h_ref[...]
    gate = jnp.dot(h, wg_ref[...], preferred_element_type=F32)
    up = jnp.dot(h, wu_ref[...], preferred_element_type=F32)
    act = (gate * jax.nn.sigmoid(gate) * up).astype(BF16)
    acc_ref[...] += jnp.dot(act, wd_ref[...], preferred_element_type=F32)

    @pl.when(j == n_f - 1)
    def _():
        y = x_ref[...] + 0.5 * acc_ref[...]
        if final:
            y = _rms(y, fg_ref[...])
        o_ref[...] = y


def _ffn(x, g, wg, wu, wd, final_g=None, tm=512, tf=1408):
    n, d = x.shape
    f = wg.shape[1]
    n_f = f // tf
    final = final_g is not None
    in_specs = [
        pl.BlockSpec((tm, d), lambda i, j: (i, 0)),
        pl.BlockSpec((1, d), lambda i, j: (0, 0)),
        pl.BlockSpec((d, tf), lambda i, j: (0, j)),
        pl.BlockSpec((d, tf), lambda i, j: (0, j)),
        pl.BlockSpec((tf, d), lambda i, j: (j, 0)),
    ]
    args = [x, g.reshape(1, d), wg, wu, wd]
    if final:
        in_specs.append(pl.BlockSpec((1, d), lambda i, j: (0, 0)))
        args.append(final_g.reshape(1, d))
    return pl.pallas_call(
        functools.partial(_ffn_kernel, n_f=n_f, final=final),
        grid=(n // tm, n_f),
        in_specs=in_specs,
        out_specs=pl.BlockSpec((tm, d), lambda i, j: (i, 0)),
        out_shape=jax.ShapeDtypeStruct((n, d), F32),
        scratch_shapes=[pltpu.VMEM((tm, d), BF16), pltpu.VMEM((tm, d), F32)],
        compiler_params=_cparams("parallel", "arbitrary"),
        name="ffn",
    )(*args)


def _inproj_kernel(x_ref, g_ref, w1, w2, w3, w4, o1, o2, o3, o4):
    h = _rms(x_ref[...], g_ref[...]).astype(BF16)
    for w, o in ((w1, o1), (w2, o2), (w3, o3), (w4, o4)):
        o[...] = jnp.dot(h, w[...], preferred_element_type=F32)


def _inproj(x, g, ws, tm=512):
    n, d = x.shape
    widths = [w.shape[1] for w in ws]
    return pl.pallas_call(
        _inproj_kernel,
        grid=(n // tm,),
        in_specs=[pl.BlockSpec((tm, d), lambda i: (i, 0)), pl.BlockSpec((1, d), lambda i: (0, 0))]
        + [pl.BlockSpec((d, wd), lambda i: (0, 0)) for wd in widths],
        out_specs=[pl.BlockSpec((tm, wd), lambda i: (i, 0)) for wd in widths],
        out_shape=[jax.ShapeDtypeStruct((n, wd), F32) for wd in widths],
        compiler_params=_cparams("parallel"),
        name="inproj",
    )(x, g.reshape(1, d), *ws)


def _conv_kernel(c_ref, w_ref, o_ref):
    t = c_ref.shape[1]
    w = GROUP_WIDTH
    u = c_ref[0, :, 2 * w:3 * w] * c_ref[0, :, 0:w]
    row = lax.broadcasted_iota(jnp.int32, u.shape, 0)
    prev = jnp.where(row == 0, 0.0, pltpu.roll(u, 1, 0))
    nxt = jnp.where(row == t - 1, 0.0, pltpu.roll(u, t - 1, 0))
    y = prev * w_ref[0:1, :] + u * w_ref[1:2, :] + nxt * w_ref[2:3, :]
    o_ref[0] = c_ref[0, :, w:2 * w] * y


def _conv(c, conv_w):
    b, t, cw = c.shape
    return pl.pallas_call(
        _conv_kernel,
        grid=(b,),
        in_specs=[pl.BlockSpec((1, t, cw), lambda i: (i, 0, 0)), pl.BlockSpec((3, GROUP_WIDTH), lambda i: (0, 0))],
        out_specs=pl.BlockSpec((1, t, GROUP_WIDTH), lambda i: (i, 0, 0)),
        out_shape=jax.ShapeDtypeStruct((b, t, GROUP_WIDTH), F32),
        compiler_params=_cparams("parallel"),
        name="conv",
    )(c, conv_w)


def _softplus(z):
    return jnp.maximum(z, 0.0) + jnp.log(1.0 + jnp.exp(-jnp.abs(z)))


def _hdot(a, b):
    return jnp.dot(a, b, precision=lax.Precision.HIGHEST, preferred_element_type=F32)


def _rwkv_prep_kernel(rw_ref, w0f, w0b, w2f, w2b, a0f, a0b, a2f, a2b, g2, kk_w, ka_w, rk_w,
                      pf_ref, pb_ref, g_ref, bonus_ref):
    w = GROUP_WIDTH
    r = rw_ref[:, 0:w]
    k = rw_ref[:, w:2 * w]
    v = rw_ref[:, 2 * w:3 * w]
    base = 3 * w
    wd = (rw_ref[:, base:base + 64], rw_ref[:, base + 64:base + 128])
    ad = (rw_ref[:, base + 128:base + 192], rw_ref[:, base + 192:base + 256])
    gd = rw_ref[:, base + 256:base + 384]
    g_ref[...] = _hdot(jax.nn.sigmoid(gd), g2[...])
    kk = k * kk_w[...]
    kk = kk / jnp.maximum(jnp.sqrt(_head_sum(kk * kk)), 1e-12)
    k_sum = jnp.zeros_like(k)
    for d, (w0, w2, a0, a2, p_ref) in enumerate(((w0f, w2f, a0f, a2f, pf_ref), (w0b, w2b, a0b, a2b, pb_ref))):
        w_log = -_softplus(-(w0[...] + _hdot(jnp.tanh(wd[d]), w2[...]))) - 0.5
        decay = jnp.exp(-jnp.exp(w_log))
        a = jax.nn.sigmoid(a0[...] + _hdot(ad[d], a2[...]))
        k_dir = k * (1.0 + (a - 1.0) * ka_w[...])
        k_sum = k_sum + k_dir
        p_ref[:, 0:w] = r
        p_ref[:, w:2 * w] = kk
        p_ref[:, 2 * w:3 * w] = v
        p_ref[:, 3 * w:4 * w] = decay
        p_ref[:, 4 * w:5 * w] = k_dir
        p_ref[:, 5 * w:6 * w] = kk * a
    bonus_ref[...] = _head_sum(r * k_sum * rk_w[...]) * v


def _rwkv_prep(rw, p, tm=512):
    n, cw = rw.shape
    w = GROUP_WIDTH
    row = lambda a: a.reshape(1, w)
    params = [row(p["w0_f"]), row(p["w0_b"]), p["w2_f"], p["w2_b"], row(p["a0_f"]), row(p["a0_b"]),
              p["a2_f"], p["a2_b"], p["g2"], row(p["k_k"]), row(p["k_a"]), row(p["r_k"])]
    full = lambda a: pl.BlockSpec(a.shape, lambda i: (0, 0))
    return pl.pallas_call(
        _rwkv_prep_kernel,
        grid=(n // tm,),
        in_specs=[pl.BlockSpec((tm, cw), lambda i: (i, 0))] + [full(a) for a in params],
        out_specs=[pl.BlockSpec((tm, 6 * w), lambda i: (i, 0)), pl.BlockSpec((tm, 6 * w), lambda i: (i, 0)),
                   pl.BlockSpec((tm, w), lambda i: (i, 0)), pl.BlockSpec((tm, w), lambda i: (i, 0))],
        out_shape=[jax.ShapeDtypeStruct((n, 6 * w), F32), jax.ShapeDtypeStruct((n, 6 * w), F32),
                   jax.ShapeDtypeStruct((n, w), F32), jax.ShapeDtypeStruct((n, w), F32)],
        compiler_params=_cparams("parallel"),
        name="rwkv_prep",
    )(rw, *params)


def _rwkv_scan_kernel(pf_ref, pb_ref, yf_ref, yb_ref, s_ref, *, nb, tb):
    w = GROUP_WIDTH
    n = HEAD_DIM
    nch = 2 * nb

    @pl.when(pl.program_id(1) == 0)
    def _():
        s_ref[...] = jnp.zeros_like(s_ref)

    ones = _head_ones(w, n).astype(BF16)
    diag1 = (lax.broadcasted_iota(jnp.int32, (n, w), 0) == lax.broadcasted_iota(jnp.int32, (n, w), 1) % n).astype(F32)
    diag = jnp.concatenate([diag1] * nch, axis=0)

    def hsum(x):
        return jnp.dot(x.astype(BF16), ones, preferred_element_type=F32)

    def step(j, carry):
        tf = j
        tr = tb - 1 - j

        def stack(col):
            rows = [jnp.broadcast_to(pf_ref[b, pl.ds(tf, 1), col * w:(col + 1) * w], (n, w)) for b in range(nb)]
            rows += [jnp.broadcast_to(pb_ref[b, pl.ds(tr, 1), col * w:(col + 1) * w], (n, w)) for b in range(nb)]
            return jnp.concatenate(rows, axis=0)

        s = s_ref[...]
        u = hsum(s * stack(1))
        vcol = hsum(stack(2) * diag)
        s = s * stack(3) - u * stack(5) + vcol * stack(4)
        s_ref[...] = s
        ybc = hsum(s * stack(0)) * diag
        for c in range(nch):
            yrow = jnp.sum(ybc[c * n:(c + 1) * n], axis=0, keepdims=True)
            if c < nb:
                yf_ref[c, pl.ds(tf, 1), :] = yrow
            else:
                yb_ref[c - nb, pl.ds(tr, 1), :] = yrow
        return carry

    lax.fori_loop(0, tb, step, 0)


def _rwkv_scan(pf, pb, nb=4, tb=128):
    b, t, cw = pf.shape
    w = GROUP_WIDTH
    nt = t // tb
    return pl.pallas_call(
        functools.partial(_rwkv_scan_kernel, nb=nb, tb=tb),
        grid=(b // nb, nt),
        in_specs=[pl.BlockSpec((nb, tb, cw), lambda i, j: (i, j, 0)),
                  pl.BlockSpec((nb, tb, cw), lambda i, j: (i, nt - 1 - j, 0))],
        out_specs=[pl.BlockSpec((nb, tb, w), lambda i, j: (i, j, 0)),
                   pl.BlockSpec((nb, tb, w), lambda i, j: (i, nt - 1 - j, 0))],
        out_shape=[jax.ShapeDtypeStruct((b, t, w), F32), jax.ShapeDtypeStruct((b, t, w), F32)],
        scratch_shapes=[pltpu.VMEM((2 * nb * HEAD_DIM, w), F32)],
        compiler_params=_cparams("parallel", "arbitrary"),
        name="rwkv_scan",
    )(pf, pb)


def _rwkv_post_kernel(yf_ref, yb_ref, g_ref, bonus_ref, lg_ref, lb_ref, o_ref):
    y = yf_ref[...] + yb_ref[...]
    mu = _head_sum(y) * (1.0 / HEAD_DIM)
    d = y - mu
    var = _head_sum(d * d) * (1.0 / HEAD_DIM)
    yn = d * lax.rsqrt(var + RWKV_GN_EPS) * lg_ref[...] + lb_ref[...]
    o_ref[...] = (yn + bonus_ref[...]) * g_ref[...]


def _rwkv_post(yf, yb, g, bonus, lnx_g, lnx_b, tm=1024):
    n, w = yf.shape
    tile = pl.BlockSpec((tm, w), lambda i: (i, 0))
    vec = pl.BlockSpec((1, w), lambda i: (0, 0))
    return pl.pallas_call(
        _rwkv_post_kernel,
        grid=(n // tm,),
        in_specs=[tile, tile, tile, tile, vec, vec],
        out_specs=tile,
        out_shape=jax.ShapeDtypeStruct((n, w), F32),
        compiler_params=_cparams("parallel"),
        name="rwkv_post",
    )(yf, yb, g, bonus, lnx_g.reshape(1, w), lnx_b.reshape(1, w))


def _ret_kernel(rq_ref, rkv_ref, posq_ref, posk_ref, gn_ref, o_ref, k_s, v_s, *, tq):
    w = GROUP_WIDTH
    n = HEAD_DIM
    t = rkv_ref.shape[1]
    qi = pl.program_id(1)

    @pl.when(qi == 0)
    def _():
        k_s[...] = _rope(rkv_ref[0, :, w:2 * w], posk_ref[0], n // 2).astype(BF16)
        v_s[...] = rkv_ref[0, :, 2 * w:3 * w].astype(BF16)

    q = (_rope(rq_ref[0, :, 0:w], posq_ref[0], n // 2) * (n ** -0.5)).astype(BF16)
    row = lax.broadcasted_iota(jnp.int32, (tq, t), 0) + qi * tq
    col = lax.broadcasted_iota(jnp.int32, (tq, t), 1)
    dist = jnp.abs(row - col).astype(F32)
    outs = []
    for h in range(GROUP_HEADS):
        sl = slice(h * n, (h + 1) * n)
        s = lax.dot_general(q[:, sl], k_s[:, sl], (((1,), (1,)), ((), ())), preferred_element_type=F32)
        s = (s * jnp.exp(dist * RET_LOG_GAMMA[h])).astype(BF16)
        outs.append(jnp.dot(s, v_s[:, sl], preferred_element_type=F32))
    o = jnp.concatenate(outs, axis=-1)
    o = o * lax.rsqrt(_head_sum(o * o) * (1.0 / n) + NORM_EPS) * gn_ref[...]
    g = rq_ref[0, :, 3 * w:4 * w]
    o_ref[0] = g * jax.nn.sigmoid(g) * o


def _ret(rt, pos, gn_g, tq=512):
    b, t, cw = rt.shape
    w = GROUP_WIDTH
    tq = min(tq, t)
    return pl.pallas_call(
        functools.partial(_ret_kernel, tq=tq),
        grid=(b, t // tq),
        in_specs=[pl.BlockSpec((1, tq, cw), lambda i, j: (i, j, 0)),
                  pl.BlockSpec((1, t, cw), lambda i, j: (i, 0, 0)),
                  pl.BlockSpec((1, tq, 1), lambda i, j: (i, j, 0)),
                  pl.BlockSpec((1, t, 1), lambda i, j: (i, 0, 0)),
                  pl.BlockSpec((1, w), lambda i, j: (0, 0))],
        out_specs=pl.BlockSpec((1, tq, w), lambda i, j: (i, j, 0)),
        out_shape=jax.ShapeDtypeStruct((b, t, w), F32),
        scratch_shapes=[pltpu.VMEM((t, w), BF16), pltpu.VMEM((t, w), BF16)],
        compiler_params=_cparams("parallel", "arbitrary"),
        name="retention",
    )(rt, rt, pos, pos, gn_g.reshape(1, w))


def _mla_kernel(mq_ref, mkv_ref, posq_ref, posk_ref, qn_ref, qb_ref, kvn_ref, kvb_ref, o_ref,
                kn_s, kr_s, v_s):
    w = GROUP_WIDTH
    qi = pl.program_id(1)
    kv0 = MLA_Q_RANK

    @pl.when(qi == 0)
    def _():
        ckv = _rms(mkv_ref[0, :, kv0:kv0 + MLA_KV_RANK], kvn_ref[...]).astype(BF16)
        kv = jnp.dot(ckv, kvb_ref[...], preferred_element_type=F32)
        kn_s[...] = kv[:, 0:w].astype(BF16)
        v_s[...] = kv[:, w:2 * w].astype(BF16)
        kr = _rope(mkv_ref[0, :, kv0 + MLA_KV_RANK:MLA_PAD], posk_ref[0], MLA_ROPE // 2)
        kr_s[...] = kr[:, 0:MLA_ROPE].astype(BF16)

    qa = _rms(mq_ref[0, :, 0:MLA_Q_RANK], qn_ref[...]).astype(BF16)
    q = jnp.dot(qa, qb_ref[...], preferred_element_type=F32)
    scale = (MLA_NOPE + MLA_ROPE) ** -0.5
    qn = (q[:, 0:w] * scale).astype(BF16)
    qr = (_rope(q[:, w:w + GROUP_HEADS * MLA_ROPE], posq_ref[0], MLA_ROPE // 2) * scale).astype(BF16)
    nt = (((1,), (1,)), ((), ()))
    outs = []
    for h in range(GROUP_HEADS):
        sl = slice(h * MLA_NOPE, (h + 1) * MLA_NOPE)
        s = lax.dot_general(qn[:, sl], kn_s[:, sl], nt, preferred_element_type=F32)
        s = s + lax.dot_general(qr[:, h * MLA_ROPE:(h + 1) * MLA_ROPE], kr_s[...], nt, preferred_element_type=F32)
        p = jnp.exp(s - jnp.max(s, axis=-1, keepdims=True))
        den = jnp.sum(p, axis=-1, keepdims=True)
        outs.append(jnp.dot(p.astype(BF16), v_s[:, sl], preferred_element_type=F32) / den)
    o_ref[0] = jnp.concatenate(outs, axis=-1)


def _mla(ml, pos, qn_g, qb, kvn_g, kvb, tq=512):
    b, t, cw = ml.shape
    w = GROUP_WIDTH
    tq = min(tq, t)
    full = lambda a: pl.BlockSpec(a.shape, lambda i, j: (0, 0))
    qn_g = qn_g.reshape(1, -1)
    kvn_g = kvn_g.reshape(1, -1)
    return pl.pallas_call(
        _mla_kernel,
        grid=(b, t // tq),
        in_specs=[pl.BlockSpec((1, tq, cw), lambda i, j: (i, j, 0)),
                  pl.BlockSpec((1, t, cw), lambda i, j: (i, 0, 0)),
                  pl.BlockSpec((1, tq, 1), lambda i, j: (i, j, 0)),
                  pl.BlockSpec((1, t, 1), lambda i, j: (i, 0, 0)),
                  full(qn_g), full(qb), full(kvn_g), full(kvb)],
        out_specs=pl.BlockSpec((1, tq, w), lambda i, j: (i, j, 0)),
        out_shape=jax.ShapeDtypeStruct((b, t, w), F32),
        scratch_shapes=[pltpu.VMEM((t, w), BF16), pltpu.VMEM((t, MLA_ROPE), BF16), pltpu.VMEM((t, w), BF16)],
        compiler_params=_cparams("parallel", "arbitrary"),
        name="mla",
    )(ml, ml, pos, pos, qn_g, qb, kvn_g, kvb)


def _outproj_kernel(x_ref, y1, y2, y3, y4, w_ref, o_ref):
    w = GROUP_WIDTH
    acc = x_ref[...]
    for g, y in enumerate((y1, y2, y3, y4)):
        acc = acc + jnp.dot(y[...].astype(BF16), w_ref[g * w:(g + 1) * w, :], preferred_element_type=F32)
    o_ref[...] = acc


def _outproj(x, ys, w_out, tm=1024):
    n, d = x.shape
    w = GROUP_WIDTH
    ytile = pl.BlockSpec((tm, w), lambda i: (i, 0))
    return pl.pallas_call(
        _outproj_kernel,
        grid=(n // tm,),
        in_specs=[pl.BlockSpec((tm, d), lambda i: (i, 0)), ytile, ytile, ytile, ytile,
                  pl.BlockSpec(w_out.shape, lambda i: (0, 0))],
        out_specs=pl.BlockSpec((tm, d), lambda i: (i, 0)),
        out_shape=jax.ShapeDtypeStruct((n, d), F32),
        compiler_params=_cparams("parallel"),
        name="outproj",
    )(x, *ys, w_out)


def _regroup_heads(wm, parts):
    k = wm.shape[0]
    wm = wm.reshape(k, GROUP_HEADS, sum(parts))
    out, off = [], 0
    for p in parts:
        out.append(wm[:, :, off:off + p].reshape(k, GROUP_HEADS * p))
        off += p
    return jnp.concatenate(out, axis=1)


def kernel(x, positions, ffn1_norm, ffn1_w_gate, ffn1_w_up, ffn1_w_down, mix_norm, w_in, w_out, conv_w, rwkv_w0_f, rwkv_w0_b, rwkv_w2_f, rwkv_w2_b, rwkv_a0_f, rwkv_a0_b, rwkv_a2_f, rwkv_a2_b, rwkv_g2, rwkv_k_k, rwkv_k_a, rwkv_r_k, rwkv_lnx_g, rwkv_lnx_b, ret_gn_g, mla_q_a_norm, mla_q_b, mla_kv_a_norm, mla_kv_b, ffn2_norm, ffn2_w_gate, ffn2_w_up, ffn2_w_down, final_norm):
    b, t, d = x.shape
    n = b * t
    w = GROUP_WIDTH
    depth = w_in.shape[0]
    pos = positions.astype(F32).reshape(b, t, 1)
    xf = x.reshape(n, d)
    for l in range(depth):
        xf = _ffn(xf, ffn1_norm[l], ffn1_w_gate[l].astype(BF16), ffn1_w_up[l].astype(BF16),
                  ffn1_w_down[l].astype(BF16))
        wl = w_in[l].astype(BF16)
        w_mla = jnp.pad(wl[:, MLA_COLS[0]:MLA_COLS[1]], ((0, 0), (0, MLA_PAD - (MLA_COLS[1] - MLA_COLS[0]))))
        ws = [wl[:, CONV_COLS[0]:CONV_COLS[1]], wl[:, RWKV_COLS[0]:RWKV_COLS[1]], wl[:, RET_COLS[0]:RET_COLS[1]], w_mla]
        cv, rw, rt, ml = _inproj(xf, mix_norm[l], ws)
        y_conv = _conv(cv.reshape(b, t, -1), conv_w[l])
        rp = dict(w0_f=rwkv_w0_f[l], w0_b=rwkv_w0_b[l], w2_f=rwkv_w2_f[l], w2_b=rwkv_w2_b[l],
                  a0_f=rwkv_a0_f[l], a0_b=rwkv_a0_b[l], a2_f=rwkv_a2_f[l], a2_b=rwkv_a2_b[l],
                  g2=rwkv_g2[l], k_k=rwkv_k_k[l], k_a=rwkv_k_a[l], r_k=rwkv_r_k[l])
        pf, pb, g, bonus = _rwkv_prep(rw, rp)
        yf, yb = _rwkv_scan(pf.reshape(b, t, -1), pb.reshape(b, t, -1))
        y_rwkv = _rwkv_post(yf.reshape(n, w), yb.reshape(n, w), g, bonus, rwkv_lnx_g[l], rwkv_lnx_b[l])
        y_ret = _ret(rt.reshape(b, t, -1), pos, ret_gn_g[l])
        qb = _regroup_heads(mla_q_b[l], (MLA_NOPE, MLA_ROPE)).astype(BF16)
        kvb = _regroup_heads(mla_kv_b[l], (MLA_NOPE, MLA_NOPE)).astype(BF16)
        y_mla = _mla(ml.reshape(b, t, -1), pos, mla_q_a_norm[l], qb, mla_kv_a_norm[l], kvb)
        xf = _outproj(xf, (y_conv.reshape(n, w), y_rwkv, y_ret.reshape(n, w), y_mla.reshape(n, w)),
                      w_out[l].astype(BF16))
        xf = _ffn(xf, ffn2_norm[l], ffn2_w_gate[l].astype(BF16), ffn2_w_up[l].astype(BF16),
                  ffn2_w_down[l].astype(BF16), final_g=final_norm if l == depth - 1 else None)
    return xf.reshape(b, t, d)
```

```python
import functools
import math

import jax
import jax.numpy as jnp
from jax import lax
from jax.experimental import pallas as pl
from jax.experimental.pallas import tpu as pltpu

F32 = jnp.float32
BF16 = jnp.bfloat16

D_MODEL = 1024
GROUP_HEADS = 4
GROUP_WIDTH = 256
HEAD_DIM = 64
D_FF = 2816
RWKV_GN_EPS = 64e-5
MLA_Q_RANK = 384
MLA_KV_RANK = 128
MLA_NOPE = 64
MLA_ROPE = 32
ROPE_BASE = 10000.0
NORM_EPS = 1e-6
RET_LOG_GAMMA = tuple(math.log(1.0 - 2.0 ** (-5.0 - h)) for h in range(GROUP_HEADS))

CONV_COLS = (0, 768)
RWKV_COLS = (768, 1920)
RET_COLS = (1920, 2944)
MLA_COLS = (2944, 3488)
MLA_PAD = 640

VMEM_LIMIT = 56 * 1024 * 1024


def _cparams(*sem):
    return pltpu.CompilerParams(dimension_semantics=sem, vmem_limit_bytes=VMEM_LIMIT)


def _rms(x, g):
    return x * lax.rsqrt(jnp.mean(x * x, axis=-1, keepdims=True) + NORM_EPS) * g


def _head_ones(width, head):
    r = lax.broadcasted_iota(jnp.int32, (width, width), 0) // head
    c = lax.broadcasted_iota(jnp.int32, (width, width), 1) // head
    return r == c


def _head_sum(x, head=HEAD_DIM):
    g = _head_ones(x.shape[-1], head).astype(F32)
    return jnp.dot(x, g, precision=lax.Precision.HIGHEST, preferred_element_type=F32)


def _rotate_half(x, half):
    n = x.shape[-1]
    lane = lax.broadcasted_iota(jnp.int32, x.shape, x.ndim - 1)
    first = (lane % (2 * half)) < half
    from_right = pltpu.roll(x, n - half, x.ndim - 1)
    from_left = pltpu.roll(x, half, x.ndim - 1)
    return jnp.where(first, -from_right, from_left)


def _rope(x, pos, half):
    lane = lax.broadcasted_iota(jnp.int32, (1, x.shape[-1]), 1)
    inv = jnp.exp((lane % half).astype(F32) * (-math.log(ROPE_BASE) / half))
    ang = pos * inv
    return x * jnp.cos(ang) + _rotate_half(x, half) * jnp.sin(ang)


def _ffn_kernel(x_ref, g_ref, wg_ref, wu_ref, wd_ref, *rest, n_f, final):
    if final:
        fg_ref, o_ref, h_ref, acc_ref = rest
    else:
        o_ref, h_ref, acc_ref = rest
    j = pl.program_id(1)

    @pl.when(j == 0)
    def _():
        h_ref[...] = _rms(x_ref[...], g_ref[...]).astype(BF16)
        acc_ref[...] = jnp.zeros_like(acc_ref)

    h = h_ref[...]
    gate = jnp.dot(h, wg_ref[...], preferred_element_type=F32)
    up = jnp.dot(h, wu_ref[...], preferred_element_type=F32)
    act = (gate * jax.nn.sigmoid(gate) * up).astype(BF16)
    acc_ref[...] += jnp.dot(act, wd_ref[...], preferred_element_type=F32)

    @pl.when(j == n_f - 1)
    def _():
        y = x_ref[...] + 0.5 * acc_ref[...]
        if final:
            y = _rms(y, fg_ref[...])
        o_ref[...] = y


def _ffn(x, g, wg, wu, wd, final_g=None, tm=512, tf=1408):
    n, d = x.shape
    f = wg.shape[1]
    n_f = f // tf
    final = final_g is not None
    in_specs = [
        pl.BlockSpec((tm, d), lambda i, j: (i, 0)),
        pl.BlockSpec((1, d), lambda i, j: (0, 0)),
        pl.BlockSpec((d, tf), lambda i, j: (0, j)),
        pl.BlockSpec((d, tf), lambda i, j: (0, j)),
        pl.BlockSpec((tf, d), lambda i, j: (j, 0)),
    ]
    args = [x, g.reshape(1, d), wg, wu, wd]
    if final:
        in_specs.append(pl.BlockSpec((1, d), lambda i, j: (0, 0)))
        args.append(final_g.reshape(1, d))
    return pl.pallas_call(
        functools.partial(_ffn_kernel, n_f=n_f, final=final),
        grid=(n // tm, n_f),
        in_specs=in_specs,
        out_specs=pl.BlockSpec((tm, d), lambda i, j: (i, 0)),
        out_shape=jax.ShapeDtypeStruct((n, d), F32),
        scratch_shapes=[pltpu.VMEM((tm, d), BF16), pltpu.VMEM((tm, d), F32)],
        compiler_params=_cparams("parallel", "arbitrary"),
        name="ffn",
    )(*args)


def _inproj_kernel(x_ref, g_ref, w1, w2, w3, w4, o1, o2, o3, o4):
    h = _rms(x_ref[...], g_ref[...]).astype(BF16)
    for w, o in ((w1, o1), (w2, o2), (w3, o3), (w4, o4)):
        o[...] = jnp.dot(h, w[...], preferred_element_type=F32)


def _inproj(x, g, ws, tm=512):
    n, d = x.shape
    widths = [w.shape[1] for w in ws]
    return pl.pallas_call(
        _inproj_kernel,
        grid=(n // tm,),
        in_specs=[pl.BlockSpec((tm, d), lambda i: (i, 0)), pl.BlockSpec((1, d), lambda i: (0, 0))]
        + [pl.BlockSpec((d, wd), lambda i: (0, 0)) for wd in widths],
        out_specs=[pl.BlockSpec((tm, wd), lambda i: (i, 0)) for wd in widths],
        out_shape=[jax.ShapeDtypeStruct((n, wd), F32) for wd in widths],
        compiler_params=_cparams("parallel"),
        name="inproj",
    )(x, g.reshape(1, d), *ws)


def _conv_kernel(c_ref, w_ref, o_ref):
    t = c_ref.shape[1]
    w = GROUP_WIDTH
    u = c_ref[0, :, 2 * w:3 * w] * c_ref[0, :, 0:w]
    row = lax.broadcasted_iota(jnp.int32, u.shape, 0)
    prev = jnp.where(row == 0, 0.0, pltpu.roll(u, 1, 0))
    nxt = jnp.where(row == t - 1, 0.0, pltpu.roll(u, t - 1, 0))
    y = prev * w_ref[0:1, :] + u * w_ref[1:2, :] + nxt * w_ref[2:3, :]
    o_ref[0] = c_ref[0, :, w:2 * w] * y


def _conv(c, conv_w):
    b, t, cw = c.shape
    return pl.pallas_call(
        _conv_kernel,
        grid=(b,),
        in_specs=[pl.BlockSpec((1, t, cw), lambda i: (i, 0, 0)), pl.BlockSpec((3, GROUP_WIDTH), lambda i: (0, 0))],
        out_specs=pl.BlockSpec((1, t, GROUP_WIDTH), lambda i: (i, 0, 0)),
        out_shape=jax.ShapeDtypeStruct((b, t, GROUP_WIDTH), F32),
        compiler_params=_cparams("parallel"),
        name="conv",
    )(c, conv_w)


def _softplus(z):
    return jnp.maximum(z, 0.0) + jnp.log(1.0 + jnp.exp(-jnp.abs(z)))


def _hdot(a, b):
    return jnp.dot(a, b, precision=lax.Precision.HIGHEST, preferred_element_type=F32)


def _rwkv_prep_kernel(rw_ref, w0f, w0b, w2f, w2b, a0f, a0b, a2f, a2b, g2, kk_w, ka_w, rk_w,
                      pf_ref, pb_ref, g_ref, bonus_ref):
    w = GROUP_WIDTH
    r = rw_ref[:, 0:w]
    k = rw_ref[:, w:2 * w]
    v = rw_ref[:, 2 * w:3 * w]
    base = 3 * w
    wd = (rw_ref[:, base:base + 64], rw_ref[:, base + 64:base + 128])
    ad = (rw_ref[:, base + 128:base + 192], rw_ref[:, base + 192:base + 256])
    gd = rw_ref[:, base + 256:base + 384]
    g_ref[...] = _hdot(jax.nn.sigmoid(gd), g2[...])
    kk = k * kk_w[...]
    kk = kk / jnp.maximum(jnp.sqrt(_head_sum(kk * kk)), 1e-12)
    k_sum = jnp.zeros_like(k)
    for d, (w0, w2, a0, a2, p_ref) in enumerate(((w0f, w2f, a0f, a2f, pf_ref), (w0b, w2b, a0b, a2b, pb_ref))):
        w_log = -_softplus(-(w0[...] + _hdot(jnp.tanh(wd[d]), w2[...]))) - 0.5
        decay = jnp.exp(-jnp.exp(w_log))
        a = jax.nn.sigmoid(a0[...] + _hdot(ad[d], a2[...]))
        k_dir = k * (1.0 + (a - 1.0) * ka_w[...])
        k_sum = k_sum + k_dir
        p_ref[:, 0:w] = r
        p_ref[:, w:2 * w] = kk
        p_ref[:, 2 * w:3 * w] = v
        p_ref[:, 3 * w:4 * w] = decay
        p_ref[:, 4 * w:5 * w] = k_dir
        p_ref[:, 5 * w:6 * w] = kk * a
    bonus_ref[...] = _head_sum(r * k_sum * rk_w[...]) * v


def _rwkv_prep(rw, p, tm=512):
    n, cw = rw.shape
    w = GROUP_WIDTH
    row = lambda a: a.reshape(1, w)
    params = [row(p["w0_f"]), row(p["w0_b"]), p["w2_f"], p["w2_b"], row(p["a0_f"]), row(p["a0_b"]),
              p["a2_f"], p["a2_b"], p["g2"], row(p["k_k"]), row(p["k_a"]), row(p["r_k"])]
    full = lambda a: pl.BlockSpec(a.shape, lambda i: (0, 0))
    return pl.pallas_call(
        _rwkv_prep_kernel,
        grid=(n // tm,),
        in_specs=[pl.BlockSpec((tm, cw), lambda i: (i, 0))] + [full(a) for a in params],
        out_specs=[pl.BlockSpec((tm, 6 * w), lambda i: (i, 0)), pl.BlockSpec((tm, 6 * w), lambda i: (i, 0)),
                   pl.BlockSpec((tm, w), lambda i: (i, 0)), pl.BlockSpec((tm, w), lambda i: (i, 0))],
        out_shape=[jax.ShapeDtypeStruct((n, 6 * w), F32), jax.ShapeDtypeStruct((n, 6 * w), F32),
                   jax.ShapeDtypeStruct((n, w), F32), jax.ShapeDtypeStruct((n, w), F32)],
        compiler_params=_cparams("parallel"),
        name="rwkv_prep",
    )(rw, *params)


SLAB = HEAD_DIM


def _rwkv_scan_kernel(pf_ref, pb_ref, yf_ref, yb_ref, s_ref, q_ref, *, nb):
    w = GROUP_WIDTH
    n = HEAD_DIM

    @pl.when(pl.program_id(0) == 0)
    def _():
        s_ref[...] = jnp.zeros_like(s_ref)
        q_ref[...] = jnp.zeros_like(q_ref)

    ones = _head_ones(w, n).astype(BF16)
    slot = lax.broadcasted_iota(jnp.int32, (n, w), 1) % n
    diag = (lax.broadcasted_iota(jnp.int32, (n, w), 0) == slot).astype(BF16)

    def chain_step(c, p_ref, y_ref, b, t, prev_slot):
        def row(col):
            return p_ref[b, pl.ds(t, 1), col * w:(col + 1) * w]

        s = s_ref[c]
        lhs = jnp.concatenate([s.astype(BF16) * row(1).astype(BF16), diag * row(2).astype(BF16), q_ref[c]], axis=0)
        res = jnp.dot(lhs, ones, preferred_element_type=F32)
        u, vcol, y_prev = res[0:n], res[n:2 * n], res[2 * n:3 * n]
        s = s * row(3) - u * row(5) + vcol * row(4)
        s_ref[c] = s
        q_ref[c] = s.astype(BF16) * row(0).astype(BF16)
        pltpu.store(y_ref.at[b, 0], y_prev, mask=slot == prev_slot)

    def step(j, carry):
        for b in range(nb):
            chain_step(b, pf_ref, yf_ref, b, j, j - 1)
            chain_step(nb + b, pb_ref, yb_ref, b, SLAB - 1 - j, SLAB - j)
        return carry

    lax.fori_loop(0, SLAB, step, 0)
    for b in range(nb):
        for c, y_ref, last in ((b, yf_ref, SLAB - 1), (nb + b, yb_ref, 0)):
            y_last = jnp.dot(q_ref[c], ones, preferred_element_type=F32)
            pltpu.store(y_ref.at[b, 0], y_last, mask=slot == last)


def _rwkv_scan(pf, pb):
    b, t, cw = pf.shape
    w = GROUP_WIDTH
    nt = t // SLAB
    out = jax.ShapeDtypeStruct((b, nt, HEAD_DIM, w), F32)
    return pl.pallas_call(
        functools.partial(_rwkv_scan_kernel, nb=b),
        grid=(nt,),
        in_specs=[pl.BlockSpec((b, SLAB, cw), lambda j: (0, j, 0)),
                  pl.BlockSpec((b, SLAB, cw), lambda j: (0, nt - 1 - j, 0))],
        out_specs=[pl.BlockSpec((b, 1, HEAD_DIM, w), lambda j: (0, j, 0, 0)),
                   pl.BlockSpec((b, 1, HEAD_DIM, w), lambda j: (0, nt - 1 - j, 0, 0))],
        out_shape=[out, out],
        scratch_shapes=[pltpu.VMEM((2 * b, HEAD_DIM, w), F32), pltpu.VMEM((2 * b, HEAD_DIM, w), BF16)],
        compiler_params=_cparams("arbitrary"),
        name="rwkv_scan",
    )(pf, pb)


def _rwkv_post_kernel(yf_ref, yb_ref, g_ref, bonus_ref, lg_ref, lb_ref, o_ref):
    n = HEAD_DIM
    for pair in range(yf_ref.shape[1] // 2):
        zs = []
        for s in (2 * pair, 2 * pair + 1):
            y = yf_ref[0, s] + yb_ref[0, s]
            d = y - jnp.mean(y, axis=0, keepdims=True)
            var = jnp.mean(d * d, axis=0, keepdims=True)
            zs.append(d * lax.rsqrt(var + RWKV_GN_EPS))
        z = jnp.concatenate(zs, axis=0)
        zt = [z[:, 0:2 * n].T, z[:, 2 * n:4 * n].T]
        for i, s in enumerate((2 * pair, 2 * pair + 1)):
            yn = jnp.concatenate([zt[h // 2][(h % 2) * n:(h % 2 + 1) * n, i * n:(i + 1) * n]
                                  for h in range(GROUP_HEADS)], axis=1)
            rows = slice(s * n, (s + 1) * n)
            o_ref[0, rows, :] = (yn * lg_ref[...] + lb_ref[...] + bonus_ref[0, rows, :]) * g_ref[0, rows, :]


def _rwkv_post(yf, yb, g, bonus, lnx_g, lnx_b, tm=512):
    b, nt, n, w = yf.shape
    tm = min(tm, nt * n)
    slabs = pl.BlockSpec((1, tm // n, n, w), lambda i, j: (i, j, 0, 0))
    tile = pl.BlockSpec((1, tm, w), lambda i, j: (i, j, 0))
    vec = pl.BlockSpec((1, w), lambda i, j: (0, 0))
    return pl.pallas_call(
        _rwkv_post_kernel,
        grid=(b, nt * n // tm),
        in_specs=[slabs, slabs, tile, tile, vec, vec],
        out_specs=tile,
        out_shape=jax.ShapeDtypeStruct((b, nt * n, w), F32),
        compiler_params=_cparams("parallel", "parallel"),
        name="rwkv_post",
    )(yf, yb, g.reshape(b, nt * n, w), bonus.reshape(b, nt * n, w), lnx_g.reshape(1, w), lnx_b.reshape(1, w))


def _ret_kernel(rq_ref, rkv_ref, posq_ref, posk_ref, gn_ref, o_ref, k_s, v_s, *, tq):
    w = GROUP_WIDTH
    n = HEAD_DIM
    t = rkv_ref.shape[1]
    qi = pl.program_id(1)

    @pl.when(qi == 0)
    def _():
        k_s[...] = _rope(rkv_ref[0, :, w:2 * w], posk_ref[0], n // 2).astype(BF16)
        v_s[...] = rkv_ref[0, :, 2 * w:3 * w].astype(BF16)

    q = (_rope(rq_ref[0, :, 0:w], posq_ref[0], n // 2) * (n ** -0.5)).astype(BF16)
    row = lax.broadcasted_iota(jnp.int32, (tq, t), 0) + qi * tq
    col = lax.broadcasted_iota(jnp.int32, (tq, t), 1)
    dist = jnp.abs(row - col).astype(F32)
    outs = []
    for h in range(GROUP_HEADS):
        sl = slice(h * n, (h + 1) * n)
        s = lax.dot_general(q[:, sl], k_s[:, sl], (((1,), (1,)), ((), ())), preferred_element_type=F32)
        s = (s * jnp.exp(dist * RET_LOG_GAMMA[h])).astype(BF16)
        outs.append(jnp.dot(s, v_s[:, sl], preferred_element_type=F32))
    o = jnp.concatenate(outs, axis=-1)
    o = o * lax.rsqrt(_head_sum(o * o) * (1.0 / n) + NORM_EPS) * gn_ref[...]
    g = rq_ref[0, :, 3 * w:4 * w]
    o_ref[0] = g * jax.nn.sigmoid(g) * o


def _ret(rt, pos, gn_g, tq=512):
    b, t, cw = rt.shape
    w = GROUP_WIDTH
    tq = min(tq, t)
    return pl.pallas_call(
        functools.partial(_ret_kernel, tq=tq),
        grid=(b, t // tq),
        in_specs=[pl.BlockSpec((1, tq, cw), lambda i, j: (i, j, 0)),
                  pl.BlockSpec((1, t, cw), lambda i, j: (i, 0, 0)),
                  pl.BlockSpec((1, tq, 1), lambda i, j: (i, j, 0)),
                  pl.BlockSpec((1, t, 1), lambda i, j: (i, 0, 0)),
                  pl.BlockSpec((1, w), lambda i, j: (0, 0))],
        out_specs=pl.BlockSpec((1, tq, w), lambda i, j: (i, j, 0)),
        out_shape=jax.ShapeDtypeStruct((b, t, w), F32),
        scratch_shapes=[pltpu.VMEM((t, w), BF16), pltpu.VMEM((t, w), BF16)],
        compiler_params=_cparams("parallel", "arbitrary"),
        name="retention",
    )(rt, rt, pos, pos, gn_g.reshape(1, w))


def _mla_kernel(mq_ref, mkv_ref, posq_ref, posk_ref, qn_ref, qb_ref, kvn_ref, kvb_ref, o_ref,
                kn_s, kr_s, v_s):
    w = GROUP_WIDTH
    qi = pl.program_id(1)
    kv0 = MLA_Q_RANK

    @pl.when(qi == 0)
    def _():
        ckv = _rms(mkv_ref[0, :, kv0:kv0 + MLA_KV_RANK], kvn_ref[...]).astype(BF16)
        kv = jnp.dot(ckv, kvb_ref[...], preferred_element_type=F32)
        kn_s[...] = kv[:, 0:w].astype(BF16)
        v_s[...] = kv[:, w:2 * w].astype(BF16)
        kr = _rope(mkv_ref[0, :, kv0 + MLA_KV_RANK:MLA_PAD], posk_ref[0], MLA_ROPE // 2)
        kr_s[...] = kr[:, 0:MLA_ROPE].astype(BF16)

    qa = _rms(mq_ref[0, :, 0:MLA_Q_RANK], qn_ref[...]).astype(BF16)
    q = jnp.dot(qa, qb_ref[...], preferred_element_type=F32)
    scale = (MLA_NOPE + MLA_ROPE) ** -0.5
    qn = (q[:, 0:w] * scale).astype(BF16)
    qr = (_rope(q[:, w:w + GROUP_HEADS * MLA_ROPE], posq_ref[0], MLA_ROPE // 2) * scale).astype(BF16)
    nt = (((1,), (1,)), ((), ()))
    outs = []
    for h in range(GROUP_HEADS):
        sl = slice(h * MLA_NOPE, (h + 1) * MLA_NOPE)
        s = lax.dot_general(qn[:, sl], kn_s[:, sl], nt, preferred_element_type=F32)
        s = s + lax.dot_general(qr[:, h * MLA_ROPE:(h + 1) * MLA_ROPE], kr_s[...], nt, preferred_element_type=F32)
        p = jnp.exp(s - jnp.max(s, axis=-1, keepdims=True))
        den = jnp.sum(p, axis=-1, keepdims=True)
        outs.append(jnp.dot(p.astype(BF16), v_s[:, sl], preferred_element_type=F32) / den)
    o_ref[0] = jnp.concatenate(outs, axis=-1)


def _mla(ml, pos, qn_g, qb, kvn_g, kvb, tq=512):
    b, t, cw = ml.shape
    w = GROUP_WIDTH
    tq = min(tq, t)
    full = lambda a: pl.BlockSpec(a.shape, lambda i, j: (0, 0))
    qn_g = qn_g.reshape(1, -1)
    kvn_g = kvn_g.reshape(1, -1)
    return pl.pallas_call(
        _mla_kernel,
        grid=(b, t // tq),
        in_specs=[pl.BlockSpec((1, tq, cw), lambda i, j: (i, j, 0)),
                  pl.BlockSpec((1, t, cw), lambda i, j: (i, 0, 0)),
                  pl.BlockSpec((1, tq, 1), lambda i, j: (i, j, 0)),
                  pl.BlockSpec((1, t, 1), lambda i, j: (i, 0, 0)),
                  full(qn_g), full(qb), full(kvn_g), full(kvb)],
        out_specs=pl.BlockSpec((1, tq, w), lambda i, j: (i, j, 0)),
        out_shape=jax.ShapeDtypeStruct((b, t, w), F32),
        scratch_shapes=[pltpu.VMEM((t, w), BF16), pltpu.VMEM((t, MLA_ROPE), BF16), pltpu.VMEM((t, w), BF16)],
        compiler_params=_cparams("parallel", "arbitrary"),
        name="mla",
    )(ml, ml, pos, pos, qn_g, qb, kvn_g, kvb)


def _outproj_kernel(x_ref, y1, y2, y3, y4, w_ref, o_ref):
    w = GROUP_WIDTH
    acc = x_ref[...]
    for g, y in enumerate((y1, y2, y3, y4)):
        acc = acc + jnp.dot(y[...].astype(BF16), w_ref[g * w:(g + 1) * w, :], preferred_element_type=F32)
    o_ref[...] = acc


def _outproj(x, ys, w_out, tm=1024):
    n, d = x.shape
    w = GROUP_WIDTH
    ytile = pl.BlockSpec((tm, w), lambda i: (i, 0))
    return pl.pallas_call(
        _outproj_kernel,
        grid=(n // tm,),
        in_specs=[pl.BlockSpec((tm, d), lambda i: (i, 0)), ytile, ytile, ytile, ytile,
                  pl.BlockSpec(w_out.shape, lambda i: (0, 0))],
        out_specs=pl.BlockSpec((tm, d), lambda i: (i, 0)),
        out_shape=jax.ShapeDtypeStruct((n, d), F32),
        compiler_params=_cparams("parallel"),
        name="outproj",
    )(x, *ys, w_out)


def _regroup_heads(wm, parts):
    k = wm.shape[0]
    wm = wm.reshape(k, GROUP_HEADS, sum(parts))
    out, off = [], 0
    for p in parts:
        out.append(wm[:, :, off:off + p].reshape(k, GROUP_HEADS * p))
        off += p
    return jnp.concatenate(out, axis=1)


def kernel(x, positions, ffn1_norm, ffn1_w_gate, ffn1_w_up, ffn1_w_down, mix_norm, w_in, w_out, conv_w, rwkv_w0_f, rwkv_w0_b, rwkv_w2_f, rwkv_w2_b, rwkv_a0_f, rwkv_a0_b, rwkv_a2_f, rwkv_a2_b, rwkv_g2, rwkv_k_k, rwkv_k_a, rwkv_r_k, rwkv_lnx_g, rwkv_lnx_b, ret_gn_g, mla_q_a_norm, mla_q_b, mla_kv_a_norm, mla_kv_b, ffn2_norm, ffn2_w_gate, ffn2_w_up, ffn2_w_down, final_norm):
    b, t, d = x.shape
    n = b * t
    w = GROUP_WIDTH
    depth = w_in.shape[0]
    pos = positions.astype(F32).reshape(b, t, 1)
    xf = x.reshape(n, d)
    for l in range(depth):
        xf = _ffn(xf, ffn1_norm[l], ffn1_w_gate[l].astype(BF16), ffn1_w_up[l].astype(BF16),
                  ffn1_w_down[l].astype(BF16))
        wl = w_in[l].astype(BF16)
        w_mla = jnp.pad(wl[:, MLA_COLS[0]:MLA_COLS[1]], ((0, 0), (0, MLA_PAD - (MLA_COLS[1] - MLA_COLS[0]))))
        ws = [wl[:, CONV_COLS[0]:CONV_COLS[1]], wl[:, RWKV_COLS[0]:RWKV_COLS[1]], wl[:, RET_COLS[0]:RET_COLS[1]], w_mla]
        cv, rw, rt, ml = _inproj(xf, mix_norm[l], ws)
        y_conv = _conv(cv.reshape(b, t, -1), conv_w[l])
        rp = dict(w0_f=rwkv_w0_f[l], w0_b=rwkv_w0_b[l], w2_f=rwkv_w2_f[l], w2_b=rwkv_w2_b[l],
                  a0_f=rwkv_a0_f[l], a0_b=rwkv_a0_b[l], a2_f=rwkv_a2_f[l], a2_b=rwkv_a2_b[l],
                  g2=rwkv_g2[l], k_k=rwkv_k_k[l], k_a=rwkv_k_a[l], r_k=rwkv_r_k[l])
        pf, pb, g, bonus = _rwkv_prep(rw, rp)
        yf, yb = _rwkv_scan(pf.reshape(b, t, -1), pb.reshape(b, t, -1))
        y_rwkv = _rwkv_post(yf, yb, g, bonus, rwkv_lnx_g[l], rwkv_lnx_b[l]).reshape(n, w)
        y_ret = _ret(rt.reshape(b, t, -1), pos, ret_gn_g[l])
        qb = _regroup_heads(mla_q_b[l], (MLA_NOPE, MLA_ROPE)).astype(BF16)
        kvb = _regroup_heads(mla_kv_b[l], (MLA_NOPE, MLA_NOPE)).astype(BF16)
        y_mla = _mla(ml.reshape(b, t, -1), pos, mla_q_a_norm[l], qb, mla_kv_a_norm[l], kvb)
        xf = _outproj(xf, (y_conv.reshape(n, w), y_rwkv, y_ret.reshape(n, w), y_mla.reshape(n, w)),
                      w_out[l].astype(BF16))
        xf = _ffn(xf, ffn2_norm[l], ffn2_w_gate[l].astype(BF16), ffn2_w_up[l].astype(BF16),
                  ffn2_w_down[l].astype(BF16), final_g=final_norm if l == depth - 1 else None)
    return xf.reshape(b, t, d)
```

```python
import functools
import math

import jax
import jax.numpy as jnp
from jax import lax
from jax.experimental import pallas as pl
from jax.experimental.pallas import tpu as pltpu

F32 = jnp.float32
BF16 = jnp.bfloat16

D_MODEL = 1024
GROUP_HEADS = 4
GROUP_WIDTH = 256
HEAD_DIM = 64
D_FF = 2816
RWKV_GN_EPS = 64e-5
MLA_Q_RANK = 384
MLA_KV_RANK = 128
MLA_NOPE = 64
MLA_ROPE = 32
ROPE_BASE = 10000.0
NORM_EPS = 1e-6
RET_LOG_GAMMA = tuple(math.log(1.0 - 2.0 ** (-5.0 - h)) for h in range(GROUP_HEADS))
RET_CHUNK = 128

CONV_COLS = (0, 768)
RWKV_COLS = (768, 1920)
RET_COLS = (1920, 2944)
MLA_COLS = (2944, 3488)
MLA_PAD = 640

VMEM_LIMIT = 56 * 1024 * 1024


def _cparams(*sem):
    return pltpu.CompilerParams(dimension_semantics=sem, vmem_limit_bytes=VMEM_LIMIT)


def _rms(x, g):
    return x * lax.rsqrt(jnp.mean(x * x, axis=-1, keepdims=True) + NORM_EPS) * g


def _head_ones(width, head):
    r = lax.broadcasted_iota(jnp.int32, (width, width), 0) // head
    c = lax.broadcasted_iota(jnp.int32, (width, width), 1) // head
    return r == c


def _head_sum(x, head=HEAD_DIM):
    g = _head_ones(x.shape[-1], head).astype(F32)
    return jnp.dot(x, g, precision=lax.Precision.HIGHEST, preferred_element_type=F32)


def _rotate_half(x, half):
    n = x.shape[-1]
    lane = lax.broadcasted_iota(jnp.int32, x.shape, x.ndim - 1)
    first = (lane % (2 * half)) < half
    from_right = pltpu.roll(x, n - half, x.ndim - 1)
    from_left = pltpu.roll(x, half, x.ndim - 1)
    return jnp.where(first, -from_right, from_left)


def _rope(x, pos, half):
    lane = lax.broadcasted_iota(jnp.int32, (1, x.shape[-1]), 1)
    inv = jnp.exp((lane % half).astype(F32) * (-math.log(ROPE_BASE) / half))
    ang = pos * inv
    return x * jnp.cos(ang) + _rotate_half(x, half) * jnp.sin(ang)


def _ffn_kernel(x_ref, g_ref, wg_ref, wu_ref, wd_ref, *rest, n_f, final):
    if final:
        fg_ref, o_ref, h_ref, acc_ref = rest
    else:
        o_ref, h_ref, acc_ref = rest
    j = pl.program_id(1)

    @pl.when(j == 0)
    def _():
        h_ref[...] = _rms(x_ref[...], g_ref[...]).astype(BF16)
        acc_ref[...] = jnp.zeros_like(acc_ref)

    h = h_ref[...]
    gate = jnp.dot(h, wg_ref[...], preferred_element_type=F32)
    up = jnp.dot(h, wu_ref[...], preferred_element_type=F32)
    act = (gate * jax.nn.sigmoid(gate) * up).astype(BF16)
    acc_ref[...] += jnp.dot(act, wd_ref[...], preferred_element_type=F32)

    @pl.when(j == n_f - 1)
    def _():
        y = x_ref[...] + 0.5 * acc_ref[...]
        if final:
            y = _rms(y, fg_ref[...])
        o_ref[...] = y


def _ffn(x, g, wg, wu, wd, final_g=None, tm=512, tf=1408):
    n, d = x.shape
    f = wg.shape[1]
    n_f = f // tf
    final = final_g is not None
    in_specs = [
        pl.BlockSpec((tm, d), lambda i, j: (i, 0)),
        pl.BlockSpec((1, d), lambda i, j: (0, 0)),
        pl.BlockSpec((d, tf), lambda i, j: (0, j)),
        pl.BlockSpec((d, tf), lambda i, j: (0, j)),
        pl.BlockSpec((tf, d), lambda i, j: (j, 0)),
    ]
    args = [x, g.reshape(1, d), wg, wu, wd]
    if final:
        in_specs.append(pl.BlockSpec((1, d), lambda i, j: (0, 0)))
        args.append(final_g.reshape(1, d))
    return pl.pallas_call(
        functools.partial(_ffn_kernel, n_f=n_f, final=final),
        grid=(n // tm, n_f),
        in_specs=in_specs,
        out_specs=pl.BlockSpec((tm, d), lambda i, j: (i, 0)),
        out_shape=jax.ShapeDtypeStruct((n, d), F32),
        scratch_shapes=[pltpu.VMEM((tm, d), BF16), pltpu.VMEM((tm, d), F32)],
        compiler_params=_cparams("parallel", "arbitrary"),
        name="ffn",
    )(*args)


def _inproj_kernel(x_ref, g_ref, w1, w2, w3, w4, o1, o2, o3, o4):
    h = _rms(x_ref[...], g_ref[...]).astype(BF16)
    for w, o in ((w1, o1), (w2, o2), (w3, o3), (w4, o4)):
        o[...] = jnp.dot(h, w[...], preferred_element_type=F32)


def _inproj(x, g, ws, tm=512):
    n, d = x.shape
    widths = [w.shape[1] for w in ws]
    return pl.pallas_call(
        _inproj_kernel,
        grid=(n // tm,),
        in_specs=[pl.BlockSpec((tm, d), lambda i: (i, 0)), pl.BlockSpec((1, d), lambda i: (0, 0))]
        + [pl.BlockSpec((d, wd), lambda i: (0, 0)) for wd in widths],
        out_specs=[pl.BlockSpec((tm, wd), lambda i: (i, 0)) for wd in widths],
        out_shape=[jax.ShapeDtypeStruct((n, wd), F32) for wd in widths],
        compiler_params=_cparams("parallel"),
        name="inproj",
    )(x, g.reshape(1, d), *ws)


def _conv_kernel(c_ref, w_ref, o_ref):
    t = c_ref.shape[1]
    w = GROUP_WIDTH
    u = c_ref[0, :, 2 * w:3 * w] * c_ref[0, :, 0:w]
    row = lax.broadcasted_iota(jnp.int32, u.shape, 0)
    prev = jnp.where(row == 0, 0.0, pltpu.roll(u, 1, 0))
    nxt = jnp.where(row == t - 1, 0.0, pltpu.roll(u, t - 1, 0))
    y = prev * w_ref[0:1, :] + u * w_ref[1:2, :] + nxt * w_ref[2:3, :]
    o_ref[0] = c_ref[0, :, w:2 * w] * y


def _conv(c, conv_w):
    b, t, cw = c.shape
    return pl.pallas_call(
        _conv_kernel,
        grid=(b,),
        in_specs=[pl.BlockSpec((1, t, cw), lambda i: (i, 0, 0)), pl.BlockSpec((3, GROUP_WIDTH), lambda i: (0, 0))],
        out_specs=pl.BlockSpec((1, t, GROUP_WIDTH), lambda i: (i, 0, 0)),
        out_shape=jax.ShapeDtypeStruct((b, t, GROUP_WIDTH), F32),
        compiler_params=_cparams("parallel"),
        name="conv",
    )(c, conv_w)


def _softplus(z):
    return jnp.maximum(z, 0.0) + jnp.log(1.0 + jnp.exp(-jnp.abs(z)))


def _hdot(a, b):
    return jnp.dot(a, b, precision=lax.Precision.HIGHEST, preferred_element_type=F32)


def _rwkv_prep_kernel(rw_ref, w0f, w0b, w2f, w2b, a0f, a0b, a2f, a2b, g2, kk_w, ka_w, rk_w,
                      pf_ref, pb_ref, g_ref, bonus_ref):
    w = GROUP_WIDTH
    r = rw_ref[:, 0:w]
    k = rw_ref[:, w:2 * w]
    v = rw_ref[:, 2 * w:3 * w]
    base = 3 * w
    wd = (rw_ref[:, base:base + 64], rw_ref[:, base + 64:base + 128])
    ad = (rw_ref[:, base + 128:base + 192], rw_ref[:, base + 192:base + 256])
    gd = rw_ref[:, base + 256:base + 384]
    g_ref[...] = _hdot(jax.nn.sigmoid(gd), g2[...])
    kk = k * kk_w[...]
    kk = kk / jnp.maximum(jnp.sqrt(_head_sum(kk * kk)), 1e-12)
    k_sum = jnp.zeros_like(k)
    for d, (w0, w2, a0, a2, p_ref) in enumerate(((w0f, w2f, a0f, a2f, pf_ref), (w0b, w2b, a0b, a2b, pb_ref))):
        w_log = -_softplus(-(w0[...] + _hdot(jnp.tanh(wd[d]), w2[...]))) - 0.5
        decay = jnp.exp(-jnp.exp(w_log))
        a = jax.nn.sigmoid(a0[...] + _hdot(ad[d], a2[...]))
        k_dir = k * (1.0 + (a - 1.0) * ka_w[...])
        k_sum = k_sum + k_dir
        p_ref[:, 0:w] = r
        p_ref[:, w:2 * w] = kk
        p_ref[:, 2 * w:3 * w] = v
        p_ref[:, 3 * w:4 * w] = decay
        p_ref[:, 4 * w:5 * w] = k_dir
        p_ref[:, 5 * w:6 * w] = kk * a
    bonus_ref[...] = _head_sum(r * k_sum * rk_w[...]) * v


def _rwkv_prep(rw, p, tm=512):
    n, cw = rw.shape
    w = GROUP_WIDTH
    row = lambda a: a.reshape(1, w)
    params = [row(p["w0_f"]), row(p["w0_b"]), p["w2_f"], p["w2_b"], row(p["a0_f"]), row(p["a0_b"]),
              p["a2_f"], p["a2_b"], p["g2"], row(p["k_k"]), row(p["k_a"]), row(p["r_k"])]
    full = lambda a: pl.BlockSpec(a.shape, lambda i: (0, 0))
    return pl.pallas_call(
        _rwkv_prep_kernel,
        grid=(n // tm,),
        in_specs=[pl.BlockSpec((tm, cw), lambda i: (i, 0))] + [full(a) for a in params],
        out_specs=[pl.BlockSpec((tm, 6 * w), lambda i: (i, 0)), pl.BlockSpec((tm, 6 * w), lambda i: (i, 0)),
                   pl.BlockSpec((tm, w), lambda i: (i, 0)), pl.BlockSpec((tm, w), lambda i: (i, 0))],
        out_shape=[jax.ShapeDtypeStruct((n, 6 * w), F32), jax.ShapeDtypeStruct((n, 6 * w), F32),
                   jax.ShapeDtypeStruct((n, w), F32), jax.ShapeDtypeStruct((n, w), F32)],
        compiler_params=_cparams("parallel"),
        name="rwkv_prep",
    )(rw, *params)


SLAB = HEAD_DIM
SCAN_GROUP = 4


def _rwkv_scan_kernel(pf_ref, pb_ref, yf_ref, yb_ref, s_ref, q_ref, *, nb):
    w = GROUP_WIDTH
    n = HEAD_DIM

    @pl.when(pl.program_id(0) == 0)
    def _():
        s_ref[...] = jnp.zeros_like(s_ref)
        q_ref[...] = jnp.zeros_like(q_ref)

    ones = _head_ones(w, n).astype(BF16)
    slot = lax.broadcasted_iota(jnp.int32, (n, w), 1) % n
    diag = (lax.broadcasted_iota(jnp.int32, (n, w), 0) == slot).astype(BF16)

    def group_step(chains):
        def row(ch, col):
            return ch[1][ch[3], pl.ds(ch[4], 1), col * w:(col + 1) * w]

        pieces = []
        for ch in chains:
            pieces += [s_ref[ch[0]].astype(BF16) * row(ch, 1).astype(BF16), diag * row(ch, 2).astype(BF16),
                       q_ref[ch[0]]]
        res = jnp.dot(jnp.concatenate(pieces, axis=0), ones, preferred_element_type=F32)
        for i, ch in enumerate(chains):
            c, _, y_ref, b, _, mask = ch
            u, vcol, y_prev = (res[(3 * i + k) * n:(3 * i + k + 1) * n] for k in range(3))
            s = s_ref[c] * row(ch, 3) - u * row(ch, 5) + vcol * row(ch, 4)
            s_ref[c] = s
            q_ref[c] = s.astype(BF16) * row(ch, 0).astype(BF16)
            pltpu.store(y_ref.at[b, 0], y_prev, mask=mask)

    def step(j, carry):
        mask_f = slot == j - 1
        mask_b = slot == SLAB - j
        chains = [(b, pf_ref, yf_ref, b, j, mask_f) for b in range(nb)]
        chains += [(nb + b, pb_ref, yb_ref, b, SLAB - 1 - j, mask_b) for b in range(nb)]
        for g in range(0, 2 * nb, SCAN_GROUP):
            group_step(chains[g:g + SCAN_GROUP])
        return carry

    lax.fori_loop(0, SLAB, step, 0, unroll=2)
    for b in range(nb):
        for c, y_ref, last in ((b, yf_ref, SLAB - 1), (nb + b, yb_ref, 0)):
            y_last = jnp.dot(q_ref[c], ones, preferred_element_type=F32)
            pltpu.store(y_ref.at[b, 0], y_last, mask=slot == last)


def _rwkv_scan(pf, pb):
    b, t, cw = pf.shape
    w = GROUP_WIDTH
    nt = t // SLAB
    out = jax.ShapeDtypeStruct((b, nt, HEAD_DIM, w), F32)
    return pl.pallas_call(
        functools.partial(_rwkv_scan_kernel, nb=b),
        grid=(nt,),
        in_specs=[pl.BlockSpec((b, SLAB, cw), lambda j: (0, j, 0)),
                  pl.BlockSpec((b, SLAB, cw), lambda j: (0, nt - 1 - j, 0))],
        out_specs=[pl.BlockSpec((b, 1, HEAD_DIM, w), lambda j: (0, j, 0, 0)),
                   pl.BlockSpec((b, 1, HEAD_DIM, w), lambda j: (0, nt - 1 - j, 0, 0))],
        out_shape=[out, out],
        scratch_shapes=[pltpu.VMEM((2 * b, HEAD_DIM, w), F32), pltpu.VMEM((2 * b, HEAD_DIM, w), BF16)],
        compiler_params=_cparams("arbitrary"),
        name="rwkv_scan",
    )(pf, pb)


def _rwkv_post_kernel(yf_ref, yb_ref, g_ref, bonus_ref, lg_ref, lb_ref, o_ref):
    n = HEAD_DIM
    for pair in range(yf_ref.shape[1] // 2):
        zs = []
        for s in (2 * pair, 2 * pair + 1):
            y = yf_ref[0, s] + yb_ref[0, s]
            d = y - jnp.mean(y, axis=0, keepdims=True)
            var = jnp.mean(d * d, axis=0, keepdims=True)
            zs.append(d * lax.rsqrt(var + RWKV_GN_EPS))
        z = jnp.concatenate(zs, axis=0)
        zt = [z[:, 0:2 * n].T, z[:, 2 * n:4 * n].T]
        for i, s in enumerate((2 * pair, 2 * pair + 1)):
            yn = jnp.concatenate([zt[h // 2][(h % 2) * n:(h % 2 + 1) * n, i * n:(i + 1) * n]
                                  for h in range(GROUP_HEADS)], axis=1)
            rows = slice(s * n, (s + 1) * n)
            o_ref[0, rows, :] = (yn * lg_ref[...] + lb_ref[...] + bonus_ref[0, rows, :]) * g_ref[0, rows, :]


def _rwkv_post(yf, yb, g, bonus, lnx_g, lnx_b, tm=512):
    b, nt, n, w = yf.shape
    tm = min(tm, nt * n)
    slabs = pl.BlockSpec((1, tm // n, n, w), lambda i, j: (i, j, 0, 0))
    tile = pl.BlockSpec((1, tm, w), lambda i, j: (i, j, 0))
    vec = pl.BlockSpec((1, w), lambda i, j: (0, 0))
    return pl.pallas_call(
        _rwkv_post_kernel,
        grid=(b, nt * n // tm),
        in_specs=[slabs, slabs, tile, tile, vec, vec],
        out_specs=tile,
        out_shape=jax.ShapeDtypeStruct((b, nt * n, w), F32),
        compiler_params=_cparams("parallel", "parallel"),
        name="rwkv_post",
    )(yf, yb, g.reshape(b, nt * n, w), bonus.reshape(b, nt * n, w), lnx_g.reshape(1, w), lnx_b.reshape(1, w))


def _ret_kernel(r_ref, pos_ref, gn_ref, o_ref, q_s, kt_s, v_s, o_s):
    w = GROUP_WIDTH
    n = HEAD_DIM
    c = RET_CHUNK
    nc = r_ref.shape[1] // c
    pos = pos_ref[0]
    q_s[...] = _rope(r_ref[0, :, 0:w], pos, n // 2) * (n ** -0.5)
    kt_s[...] = _rope(r_ref[0, :, w:2 * w], pos, n // 2).T
    v_s[...] = r_ref[0, :, 2 * w:3 * w].astype(BF16)
    dist = jnp.abs(lax.broadcasted_iota(jnp.int32, (c, c), 0) - lax.broadcasted_iota(jnp.int32, (c, c), 1)).astype(F32)
    i_col = lax.broadcasted_iota(jnp.int32, (c, n), 0).astype(F32)
    j_row = lax.broadcasted_iota(jnp.int32, (n, c), 1).astype(F32)
    for h in range(GROUP_HEADS):
        lg = RET_LOG_GAMMA[h]
        hs = slice(h * n, (h + 1) * n)
        chunk_decay = math.exp(lg * c)
        gam = jnp.exp(lg * dist)
        state = jnp.zeros((n, n), F32)
        for ch in range(nc):
            ts = slice(ch * c, (ch + 1) * c)
            qc, kc, vc = q_s[ts, hs], kt_s[hs, ts], v_s[ts, hs]
            s = jnp.dot(qc.astype(BF16), kc.astype(BF16), preferred_element_type=F32) * gam
            o = jnp.dot(s.astype(BF16), vc, preferred_element_type=F32)
            o_s[ts, hs] = o + jnp.dot((qc * jnp.exp(lg * i_col)).astype(BF16), state.astype(BF16),
                                      preferred_element_type=F32)
            state = state * chunk_decay + jnp.dot((kc * jnp.exp(lg * (c - j_row))).astype(BF16), vc,
                                                  preferred_element_type=F32)
        state = jnp.zeros((n, n), F32)
        for ch in reversed(range(nc)):
            ts = slice(ch * c, (ch + 1) * c)
            qc, kc, vc = q_s[ts, hs], kt_s[hs, ts], v_s[ts, hs]
            o_s[ts, hs] += jnp.dot((qc * jnp.exp(lg * (c - i_col))).astype(BF16), state.astype(BF16),
                                   preferred_element_type=F32)
            state = state * chunk_decay + jnp.dot((kc * jnp.exp(lg * j_row)).astype(BF16), vc,
                                                  preferred_element_type=F32)
    o = o_s[...]
    o = o * lax.rsqrt(_head_sum(o * o) * (1.0 / n) + NORM_EPS) * gn_ref[...]
    g = r_ref[0, :, 3 * w:4 * w]
    o_ref[0] = g * jax.nn.sigmoid(g) * o


def _ret(rt, pos, gn_g):
    b, t, cw = rt.shape
    w = GROUP_WIDTH
    return pl.pallas_call(
        _ret_kernel,
        grid=(b,),
        in_specs=[pl.BlockSpec((1, t, cw), lambda i: (i, 0, 0)),
                  pl.BlockSpec((1, t, 1), lambda i: (i, 0, 0)),
                  pl.BlockSpec((1, w), lambda i: (0, 0))],
        out_specs=pl.BlockSpec((1, t, w), lambda i: (i, 0, 0)),
        out_shape=jax.ShapeDtypeStruct((b, t, w), F32),
        scratch_shapes=[pltpu.VMEM((t, w), F32), pltpu.VMEM((w, t), F32), pltpu.VMEM((t, w), BF16),
                        pltpu.VMEM((t, w), F32)],
        compiler_params=_cparams("parallel"),
        name="retention",
    )(rt, pos, gn_g.reshape(1, w))


def _mla_kernel(mq_ref, mkv_ref, posq_ref, posk_ref, qn_ref, qb_ref, kvn_ref, kvb_ref, o_ref,
                kn_s, kr_s, v_s):
    w = GROUP_WIDTH
    qi = pl.program_id(1)
    kv0 = MLA_Q_RANK

    @pl.when(qi == 0)
    def _():
        ckv = _rms(mkv_ref[0, :, kv0:kv0 + MLA_KV_RANK], kvn_ref[...]).astype(BF16)
        kv = jnp.dot(ckv, kvb_ref[...], preferred_element_type=F32)
        kn_s[...] = kv[:, 0:w].astype(BF16)
        v_s[...] = kv[:, w:2 * w].astype(BF16)
        kr = _rope(mkv_ref[0, :, kv0 + MLA_KV_RANK:MLA_PAD], posk_ref[0], MLA_ROPE // 2)
        kr_s[...] = kr[:, 0:MLA_ROPE].astype(BF16)

    qa = _rms(mq_ref[0, :, 0:MLA_Q_RANK], qn_ref[...]).astype(BF16)
    q = jnp.dot(qa, qb_ref[...], preferred_element_type=F32)
    scale = (MLA_NOPE + MLA_ROPE) ** -0.5
    qn = (q[:, 0:w] * scale).astype(BF16)
    qr = (_rope(q[:, w:w + GROUP_HEADS * MLA_ROPE], posq_ref[0], MLA_ROPE // 2) * scale).astype(BF16)
    nt = (((1,), (1,)), ((), ()))
    outs = []
    for h in range(GROUP_HEADS):
        sl = slice(h * MLA_NOPE, (h + 1) * MLA_NOPE)
        s = lax.dot_general(qn[:, sl], kn_s[:, sl], nt, preferred_element_type=F32)
        s = s + lax.dot_general(qr[:, h * MLA_ROPE:(h + 1) * MLA_ROPE], kr_s[...], nt, preferred_element_type=F32)
        p = jnp.exp(s - jnp.max(s, axis=-1, keepdims=True))
        den = jnp.sum(p, axis=-1, keepdims=True)
        outs.append(jnp.dot(p.astype(BF16), v_s[:, sl], preferred_element_type=F32) / den)
    o_ref[0] = jnp.concatenate(outs, axis=-1)


def _mla(ml, pos, qn_g, qb, kvn_g, kvb, tq=512):
    b, t, cw = ml.shape
    w = GROUP_WIDTH
    tq = min(tq, t)
    full = lambda a: pl.BlockSpec(a.shape, lambda i, j: (0, 0))
    qn_g = qn_g.reshape(1, -1)
    kvn_g = kvn_g.reshape(1, -1)
    return pl.pallas_call(
        _mla_kernel,
        grid=(b, t // tq),
        in_specs=[pl.BlockSpec((1, tq, cw), lambda i, j: (i, j, 0)),
                  pl.BlockSpec((1, t, cw), lambda i, j: (i, 0, 0)),
                  pl.BlockSpec((1, tq, 1), lambda i, j: (i, j, 0)),
                  pl.BlockSpec((1, t, 1), lambda i, j: (i, 0, 0)),
                  full(qn_g), full(qb), full(kvn_g), full(kvb)],
        out_specs=pl.BlockSpec((1, tq, w), lambda i, j: (i, j, 0)),
        out_shape=jax.ShapeDtypeStruct((b, t, w), F32),
        scratch_shapes=[pltpu.VMEM((t, w), BF16), pltpu.VMEM((t, MLA_ROPE), BF16), pltpu.VMEM((t, w), BF16)],
        compiler_params=_cparams("parallel", "arbitrary"),
        name="mla",
    )(ml, ml, pos, pos, qn_g, qb, kvn_g, kvb)


def _outproj_kernel(x_ref, y1, y2, y3, y4, w_ref, o_ref):
    w = GROUP_WIDTH
    acc = x_ref[...]
    for g, y in enumerate((y1, y2, y3, y4)):
        acc = acc + jnp.dot(y[...].astype(BF16), w_ref[g * w:(g + 1) * w, :], preferred_element_type=F32)
    o_ref[...] = acc


def _outproj(x, ys, w_out, tm=1024):
    n, d = x.shape
    w = GROUP_WIDTH
    ytile = pl.BlockSpec((tm, w), lambda i: (i, 0))
    return pl.pallas_call(
        _outproj_kernel,
        grid=(n // tm,),
        in_specs=[pl.BlockSpec((tm, d), lambda i: (i, 0)), ytile, ytile, ytile, ytile,
                  pl.BlockSpec(w_out.shape, lambda i: (0, 0))],
        out_specs=pl.BlockSpec((tm, d), lambda i: (i, 0)),
        out_shape=jax.ShapeDtypeStruct((n, d), F32),
        compiler_params=_cparams("parallel"),
        name="outproj",
    )(x, *ys, w_out)


def _regroup_heads(wm, parts):
    k = wm.shape[0]
    wm = wm.reshape(k, GROUP_HEADS, sum(parts))
    out, off = [], 0
    for p in parts:
        out.append(wm[:, :, off:off + p].reshape(k, GROUP_HEADS * p))
        off += p
    return jnp.concatenate(out, axis=1)


def kernel(x, positions, ffn1_norm, ffn1_w_gate, ffn1_w_up, ffn1_w_down, mix_norm, w_in, w_out, conv_w, rwkv_w0_f, rwkv_w0_b, rwkv_w2_f, rwkv_w2_b, rwkv_a0_f, rwkv_a0_b, rwkv_a2_f, rwkv_a2_b, rwkv_g2, rwkv_k_k, rwkv_k_a, rwkv_r_k, rwkv_lnx_g, rwkv_lnx_b, ret_gn_g, mla_q_a_norm, mla_q_b, mla_kv_a_norm, mla_kv_b, ffn2_norm, ffn2_w_gate, ffn2_w_up, ffn2_w_down, final_norm):
    b, t, d = x.shape
    n = b * t
    w = GROUP_WIDTH
    depth = w_in.shape[0]
    pos = positions.astype(F32).reshape(b, t, 1)
    xf = x.reshape(n, d)
    for l in range(depth):
        xf = _ffn(xf, ffn1_norm[l], ffn1_w_gate[l].astype(BF16), ffn1_w_up[l].astype(BF16),
                  ffn1_w_down[l].astype(BF16))
        wl = w_in[l].astype(BF16)
        w_mla = jnp.pad(wl[:, MLA_COLS[0]:MLA_COLS[1]], ((0, 0), (0, MLA_PAD - (MLA_COLS[1] - MLA_COLS[0]))))
        ws = [wl[:, CONV_COLS[0]:CONV_COLS[1]], wl[:, RWKV_COLS[0]:RWKV_COLS[1]], wl[:, RET_COLS[0]:RET_COLS[1]], w_mla]
        cv, rw, rt, ml = _inproj(xf, mix_norm[l], ws)
        y_conv = _conv(cv.reshape(b, t, -1), conv_w[l])
        rp = dict(w0_f=rwkv_w0_f[l], w0_b=rwkv_w0_b[l], w2_f=rwkv_w2_f[l], w2_b=rwkv_w2_b[l],
                  a0_f=rwkv_a0_f[l], a0_b=rwkv_a0_b[l], a2_f=rwkv_a2_f[l], a2_b=rwkv_a2_b[l],
                  g2=rwkv_g2[l], k_k=rwkv_k_k[l], k_a=rwkv_k_a[l], r_k=rwkv_r_k[l])
        pf, pb, g, bonus = _rwkv_prep(rw, rp)
        yf, yb = _rwkv_scan(pf.reshape(b, t, -1), pb.reshape(b, t, -1))
        y_rwkv = _rwkv_post(yf, yb, g, bonus, rwkv_lnx_g[l], rwkv_lnx_b[l]).reshape(n, w)
        y_ret = _ret(rt.reshape(b, t, -1), pos, ret_gn_g[l])
        qb = _regroup_heads(mla_q_b[l], (MLA_NOPE, MLA_ROPE)).astype(BF16)
        kvb = _regroup_heads(mla_kv_b[l], (MLA_NOPE, MLA_NOPE)).astype(BF16)
        y_mla = _mla(ml.reshape(b, t, -1), pos, mla_q_a_norm[l], qb, mla_kv_a_norm[l], kvb)
        xf = _outproj(xf, (y_conv.reshape(n, w), y_rwkv, y_ret.reshape(n, w), y_mla.reshape(n, w)),
                      w_out[l].astype(BF16))
        xf = _ffn(xf, ffn2_norm[l], ffn2_w_gate[l].astype(BF16), ffn2_w_up[l].astype(BF16),
                  ffn2_w_down[l].astype(BF16), final_g=final_norm if l == depth - 1 else None)
    return xf.reshape(b, t, d)
```

```python
import functools
import math

import jax
import jax.numpy as jnp
from jax import lax
from jax.experimental import pallas as pl
from jax.experimental.pallas import tpu as pltpu

F32 = jnp.float32
BF16 = jnp.bfloat16

D_MODEL = 1024
GROUP_HEADS = 4
GROUP_WIDTH = 256
HEAD_DIM = 64
D_FF = 2816
RWKV_GN_EPS = 64e-5
MLA_Q_RANK = 384
MLA_KV_RANK = 128
MLA_NOPE = 64
MLA_ROPE = 32
ROPE_BASE = 10000.0
NORM_EPS = 1e-6
RET_LOG_GAMMA = tuple(math.log(1.0 - 2.0 ** (-5.0 - h)) for h in range(GROUP_HEADS))
RET_CHUNK = 128

CONV_COLS = (0, 768)
RWKV_COLS = (768, 1920)
RET_COLS = (1920, 2944)
MLA_COLS = (2944, 3488)
MLA_PAD = 640

VMEM_LIMIT = 56 * 1024 * 1024
LANES = 128
MXU_TILE = 256


def _cparams(*sem):
    return pltpu.CompilerParams(dimension_semantics=sem, vmem_limit_bytes=VMEM_LIMIT)


def _rms(x, g):
    return x * lax.rsqrt(jnp.mean(x * x, axis=-1, keepdims=True) + NORM_EPS) * g


def _head_ones(width, head):
    r = lax.broadcasted_iota(jnp.int32, (width, width), 0) // head
    c = lax.broadcasted_iota(jnp.int32, (width, width), 1) // head
    return r == c


def _split_bf16(x):
    hi = x.astype(BF16)
    return hi, (x - hi.astype(F32)).astype(BF16)


def _head_sum(x, head=HEAD_DIM):
    g = _head_ones(x.shape[-1], head).astype(BF16)
    hi, lo = _split_bf16(x)
    return jnp.dot(hi, g, preferred_element_type=F32) + jnp.dot(lo, g, preferred_element_type=F32)


def _rotate_half(x, half):
    n = x.shape[-1]
    lane = lax.broadcasted_iota(jnp.int32, x.shape, x.ndim - 1)
    first = (lane % (2 * half)) < half
    from_right = pltpu.roll(x, n - half, x.ndim - 1)
    from_left = pltpu.roll(x, half, x.ndim - 1)
    return jnp.where(first, -from_right, from_left)


def _rope_tables(pos, half):
    lane = lax.broadcasted_iota(jnp.int32, (1, LANES), 1)
    inv = jnp.exp((lane % half).astype(F32) * (-math.log(ROPE_BASE) / half))
    ang = pos * inv
    return jnp.cos(ang), jnp.sin(ang)


def _tile_lanes(x, width):
    return x if width == x.shape[-1] else jnp.concatenate([x] * (width // x.shape[-1]), axis=-1)


def _apply_rope(x, cos, sin, half):
    width = x.shape[-1]
    return x * _tile_lanes(cos, width) + _rotate_half(x, half) * _tile_lanes(sin, width)


def _ffn_kernel(x_ref, g_ref, wg_ref, wu_ref, wd_ref, *rest, chunks, final):
    if final:
        fg_ref, o_ref = rest
    else:
        (o_ref,) = rest
    x = x_ref[...]
    h = _rms(x, g_ref[...]).astype(BF16)
    acc = None
    f0 = 0
    for fc in chunks:
        cols = slice(f0, f0 + fc)
        gate = jnp.dot(h, wg_ref[:, cols], preferred_element_type=F32)
        up = jnp.dot(h, wu_ref[:, cols], preferred_element_type=F32)
        act = (gate * jax.nn.sigmoid(gate) * up).astype(BF16)
        part = jnp.dot(act, wd_ref[cols, :], preferred_element_type=F32)
        acc = part if acc is None else acc + part
        f0 += fc
    y = x + 0.5 * acc
    if final:
        y = _rms(y, fg_ref[...])
    o_ref[...] = y


def _ffn(x, g, wg, wu, wd, final_g=None, tm=512, max_chunk=6 * MXU_TILE):
    n, d = x.shape
    f = wg.shape[1]
    chunks = [max_chunk] * (f // max_chunk) + ([f % max_chunk] if f % max_chunk else [])
    final = final_g is not None
    resident = lambda shape: pl.BlockSpec(shape, lambda i: (0, 0), pipeline_mode=pl.Buffered(1))
    in_specs = [
        pl.BlockSpec((tm, d), lambda i: (i, 0)),
        pl.BlockSpec((1, d), lambda i: (0, 0)),
        resident((d, f)), resident((d, f)), resident((f, d)),
    ]
    args = [x, g.reshape(1, d), wg, wu, wd]
    if final:
        in_specs.append(pl.BlockSpec((1, d), lambda i: (0, 0)))
        args.append(final_g.reshape(1, d))
    return pl.pallas_call(
        functools.partial(_ffn_kernel, chunks=tuple(chunks), final=final),
        grid=(n // tm,),
        in_specs=in_specs,
        out_specs=pl.BlockSpec((tm, d), lambda i: (i, 0)),
        out_shape=jax.ShapeDtypeStruct((n, d), F32),
        compiler_params=_cparams("parallel"),
        name="ffn",
    )(*args)


def _inproj_kernel(x_ref, g_ref, w1, w2, w3, w4, o1, o2, o3, o4):
    h = _rms(x_ref[...], g_ref[...]).astype(BF16)
    for w, o in ((w1, o1), (w2, o2), (w3, o3), (w4, o4)):
        o[...] = jnp.dot(h, w[...], preferred_element_type=F32)


def _inproj(x, g, ws, tm=512):
    n, d = x.shape
    widths = [w.shape[1] for w in ws]
    return pl.pallas_call(
        _inproj_kernel,
        grid=(n // tm,),
        in_specs=[pl.BlockSpec((tm, d), lambda i: (i, 0)), pl.BlockSpec((1, d), lambda i: (0, 0))]
        + [pl.BlockSpec((d, wd), lambda i: (0, 0)) for wd in widths],
        out_specs=[pl.BlockSpec((tm, wd), lambda i: (i, 0)) for wd in widths],
        out_shape=[jax.ShapeDtypeStruct((n, wd), F32) for wd in widths],
        compiler_params=_cparams("parallel"),
        name="inproj",
    )(x, g.reshape(1, d), *ws)


def _conv_kernel(c_ref, w_ref, o_ref):
    t = c_ref.shape[1]
    w = GROUP_WIDTH
    u = c_ref[0, :, 2 * w:3 * w] * c_ref[0, :, 0:w]
    row = lax.broadcasted_iota(jnp.int32, u.shape, 0)
    prev = jnp.where(row == 0, 0.0, pltpu.roll(u, 1, 0))
    nxt = jnp.where(row == t - 1, 0.0, pltpu.roll(u, t - 1, 0))
    y = prev * w_ref[0:1, :] + u * w_ref[1:2, :] + nxt * w_ref[2:3, :]
    o_ref[0] = c_ref[0, :, w:2 * w] * y


def _conv(c, conv_w):
    b, t, cw = c.shape
    return pl.pallas_call(
        _conv_kernel,
        grid=(b,),
        in_specs=[pl.BlockSpec((1, t, cw), lambda i: (i, 0, 0)), pl.BlockSpec((3, GROUP_WIDTH), lambda i: (0, 0))],
        out_specs=pl.BlockSpec((1, t, GROUP_WIDTH), lambda i: (i, 0, 0)),
        out_shape=jax.ShapeDtypeStruct((b, t, GROUP_WIDTH), F32),
        compiler_params=_cparams("parallel"),
        name="conv",
    )(c, conv_w)


def _softplus(z):
    return jnp.maximum(z, 0.0) + jnp.log(1.0 + jnp.exp(-jnp.abs(z)))


def _hdot(a, b):
    ah, al = _split_bf16(a)
    bh, bl = _split_bf16(b)
    dot = functools.partial(jnp.dot, preferred_element_type=F32)
    return dot(ah, bh) + (dot(ah, bl) + dot(al, bh))


def _rwkv_prep_kernel(rw_ref, w0f, w0b, w2f, w2b, a0f, a0b, a2f, a2b, g2, kk_w, ka_w, rk_w,
                      pf_ref, pb_ref, g_ref, bonus_ref):
    w = GROUP_WIDTH
    r = rw_ref[:, 0:w]
    k = rw_ref[:, w:2 * w]
    v = rw_ref[:, 2 * w:3 * w]
    base = 3 * w
    wd = (rw_ref[:, base:base + 64], rw_ref[:, base + 64:base + 128])
    ad = (rw_ref[:, base + 128:base + 192], rw_ref[:, base + 192:base + 256])
    gd = rw_ref[:, base + 256:base + 384]
    g_ref[...] = _hdot(jax.nn.sigmoid(gd), g2[...])
    kk = k * kk_w[...]
    kk = kk / jnp.maximum(jnp.sqrt(_head_sum(kk * kk)), 1e-12)
    k_sum = jnp.zeros_like(k)
    for d, (w0, w2, a0, a2, p_ref) in enumerate(((w0f, w2f, a0f, a2f, pf_ref), (w0b, w2b, a0b, a2b, pb_ref))):
        w_log = -_softplus(-(w0[...] + _hdot(jnp.tanh(wd[d]), w2[...]))) - 0.5
        decay = jnp.exp(-jnp.exp(w_log))
        a = jax.nn.sigmoid(a0[...] + _hdot(ad[d], a2[...]))
        k_dir = k * (1.0 + (a - 1.0) * ka_w[...])
        k_sum = k_sum + k_dir
        p_ref[:, 0:w] = r
        p_ref[:, w:2 * w] = kk
        p_ref[:, 2 * w:3 * w] = v
        p_ref[:, 3 * w:4 * w] = decay
        p_ref[:, 4 * w:5 * w] = k_dir
        p_ref[:, 5 * w:6 * w] = kk * a
    bonus_ref[...] = _head_sum(r * k_sum * rk_w[...]) * v


def _rwkv_prep(rw, p, tm=512):
    n, cw = rw.shape
    w = GROUP_WIDTH
    row = lambda a: a.reshape(1, w)
    params = [row(p["w0_f"]), row(p["w0_b"]), p["w2_f"], p["w2_b"], row(p["a0_f"]), row(p["a0_b"]),
              p["a2_f"], p["a2_b"], p["g2"], row(p["k_k"]), row(p["k_a"]), row(p["r_k"])]
    full = lambda a: pl.BlockSpec(a.shape, lambda i: (0, 0))
    return pl.pallas_call(
        _rwkv_prep_kernel,
        grid=(n // tm,),
        in_specs=[pl.BlockSpec((tm, cw), lambda i: (i, 0))] + [full(a) for a in params],
        out_specs=[pl.BlockSpec((tm, 6 * w), lambda i: (i, 0)), pl.BlockSpec((tm, 6 * w), lambda i: (i, 0)),
                   pl.BlockSpec((tm, w), lambda i: (i, 0)), pl.BlockSpec((tm, w), lambda i: (i, 0))],
        out_shape=[jax.ShapeDtypeStruct((n, 6 * w), F32), jax.ShapeDtypeStruct((n, 6 * w), F32),
                   jax.ShapeDtypeStruct((n, w), F32), jax.ShapeDtypeStruct((n, w), F32)],
        compiler_params=_cparams("parallel"),
        name="rwkv_prep",
    )(rw, *params)


SLAB = HEAD_DIM
SCAN_GROUP = 4


def _rwkv_scan_kernel(pf_ref, pb_ref, yf_ref, yb_ref, s_ref, q_ref, *, nb):
    w = GROUP_WIDTH
    n = HEAD_DIM

    @pl.when(pl.program_id(0) == 0)
    def _():
        s_ref[...] = jnp.zeros_like(s_ref)
        q_ref[...] = jnp.zeros_like(q_ref)

    ones = _head_ones(w, n).astype(BF16)
    slot = lax.broadcasted_iota(jnp.int32, (n, w), 1) % n
    diag = (lax.broadcasted_iota(jnp.int32, (n, w), 0) == slot).astype(BF16)

    def group_step(chains):
        def row(ch, col):
            return ch[1][ch[3], pl.ds(ch[4], 1), col * w:(col + 1) * w]

        pieces = []
        for ch in chains:
            pieces += [s_ref[ch[0]].astype(BF16) * row(ch, 1).astype(BF16), diag * row(ch, 2).astype(BF16),
                       q_ref[ch[0]]]
        res = jnp.dot(jnp.concatenate(pieces, axis=0), ones, preferred_element_type=F32)
        for i, ch in enumerate(chains):
            c, _, y_ref, b, _, mask = ch
            u, vcol, y_prev = (res[(3 * i + k) * n:(3 * i + k + 1) * n] for k in range(3))
            s = s_ref[c] * row(ch, 3) - u * row(ch, 5) + vcol * row(ch, 4)
            s_ref[c] = s
            q_ref[c] = s.astype(BF16) * row(ch, 0).astype(BF16)
            pltpu.store(y_ref.at[b, 0], y_prev, mask=mask)

    def step(j, carry):
        mask_f = slot == j - 1
        mask_b = slot == SLAB - j
        chains = [(b, pf_ref, yf_ref, b, j, mask_f) for b in range(nb)]
        chains += [(nb + b, pb_ref, yb_ref, b, SLAB - 1 - j, mask_b) for b in range(nb)]
        for g in range(0, 2 * nb, SCAN_GROUP):
            group_step(chains[g:g + SCAN_GROUP])
        return carry

    lax.fori_loop(0, SLAB, step, 0, unroll=2)
    for b in range(nb):
        for c, y_ref, last in ((b, yf_ref, SLAB - 1), (nb + b, yb_ref, 0)):
            y_last = jnp.dot(q_ref[c], ones, preferred_element_type=F32)
            pltpu.store(y_ref.at[b, 0], y_last, mask=slot == last)


def _rwkv_scan(pf, pb):
    b, t, cw = pf.shape
    w = GROUP_WIDTH
    nt = t // SLAB
    out = jax.ShapeDtypeStruct((b, nt, HEAD_DIM, w), F32)
    return pl.pallas_call(
        functools.partial(_rwkv_scan_kernel, nb=b),
        grid=(nt,),
        in_specs=[pl.BlockSpec((b, SLAB, cw), lambda j: (0, j, 0)),
                  pl.BlockSpec((b, SLAB, cw), lambda j: (0, nt - 1 - j, 0))],
        out_specs=[pl.BlockSpec((b, 1, HEAD_DIM, w), lambda j: (0, j, 0, 0)),
                   pl.BlockSpec((b, 1, HEAD_DIM, w), lambda j: (0, nt - 1 - j, 0, 0))],
        out_shape=[out, out],
        scratch_shapes=[pltpu.VMEM((2 * b, HEAD_DIM, w), F32), pltpu.VMEM((2 * b, HEAD_DIM, w), BF16)],
        compiler_params=_cparams("arbitrary"),
        name="rwkv_scan",
    )(pf, pb)


def _rwkv_post_kernel(yf_ref, yb_ref, g_ref, bonus_ref, lg_ref, lb_ref, o_ref):
    n = HEAD_DIM
    for pair in range(yf_ref.shape[1] // 2):
        zs = []
        for s in (2 * pair, 2 * pair + 1):
            y = yf_ref[0, s] + yb_ref[0, s]
            d = y - jnp.mean(y, axis=0, keepdims=True)
            var = jnp.mean(d * d, axis=0, keepdims=True)
            zs.append(d * lax.rsqrt(var + RWKV_GN_EPS))
        z = jnp.concatenate(zs, axis=0)
        zt = [z[:, 0:2 * n].T, z[:, 2 * n:4 * n].T]
        for i, s in enumerate((2 * pair, 2 * pair + 1)):
            yn = jnp.concatenate([zt[h // 2][(h % 2) * n:(h % 2 + 1) * n, i * n:(i + 1) * n]
                                  for h in range(GROUP_HEADS)], axis=1)
            rows = slice(s * n, (s + 1) * n)
            o_ref[0, rows, :] = (yn * lg_ref[...] + lb_ref[...] + bonus_ref[0, rows, :]) * g_ref[0, rows, :]


def _rwkv_post(yf, yb, g, bonus, lnx_g, lnx_b, tm=512):
    b, nt, n, w = yf.shape
    tm = min(tm, nt * n)
    slabs = pl.BlockSpec((1, tm // n, n, w), lambda i, j: (i, j, 0, 0))
    tile = pl.BlockSpec((1, tm, w), lambda i, j: (i, j, 0))
    vec = pl.BlockSpec((1, w), lambda i, j: (0, 0))
    return pl.pallas_call(
        _rwkv_post_kernel,
        grid=(b, nt * n // tm),
        in_specs=[slabs, slabs, tile, tile, vec, vec],
        out_specs=tile,
        out_shape=jax.ShapeDtypeStruct((b, nt * n, w), F32),
        compiler_params=_cparams("parallel", "parallel"),
        name="rwkv_post",
    )(yf, yb, g.reshape(b, nt * n, w), bonus.reshape(b, nt * n, w), lnx_g.reshape(1, w), lnx_b.reshape(1, w))


def _ret_kernel(r_ref, pos_ref, gn_ref, o_ref, q_s, kt_s, v_s, o_s):
    w = GROUP_WIDTH
    n = HEAD_DIM
    c = RET_CHUNK
    nc = r_ref.shape[1] // c
    cos, sin = _rope_tables(pos_ref[0], n // 2)
    q_s[...] = _apply_rope(r_ref[0, :, 0:w], cos, sin, n // 2) * (n ** -0.5)
    kt_s[...] = _apply_rope(r_ref[0, :, w:2 * w], cos, sin, n // 2).T
    v_s[...] = r_ref[0, :, 2 * w:3 * w].astype(BF16)
    dist = jnp.abs(lax.broadcasted_iota(jnp.int32, (c, c), 0) - lax.broadcasted_iota(jnp.int32, (c, c), 1)).astype(F32)
    i_col = lax.broadcasted_iota(jnp.int32, (c, n), 0).astype(F32)
    j_row = lax.broadcasted_iota(jnp.int32, (n, c), 1).astype(F32)
    for h in range(GROUP_HEADS):
        lg = RET_LOG_GAMMA[h]
        hs = slice(h * n, (h + 1) * n)
        chunk_decay = math.exp(lg * c)
        gam = jnp.exp(lg * dist)
        state = jnp.zeros((n, n), F32)
        for ch in range(nc):
            ts = slice(ch * c, (ch + 1) * c)
            qc, kc, vc = q_s[ts, hs], kt_s[hs, ts], v_s[ts, hs]
            s = jnp.dot(qc.astype(BF16), kc.astype(BF16), preferred_element_type=F32) * gam
            o = jnp.dot(s.astype(BF16), vc, preferred_element_type=F32)
            o_s[ts, hs] = o + jnp.dot((qc * jnp.exp(lg * i_col)).astype(BF16), state.astype(BF16),
                                      preferred_element_type=F32)
            state = state * chunk_decay + jnp.dot((kc * jnp.exp(lg * (c - j_row))).astype(BF16), vc,
                                                  preferred_element_type=F32)
        state = jnp.zeros((n, n), F32)
        for ch in reversed(range(nc)):
            ts = slice(ch * c, (ch + 1) * c)
            qc, kc, vc = q_s[ts, hs], kt_s[hs, ts], v_s[ts, hs]
            o_s[ts, hs] += jnp.dot((qc * jnp.exp(lg * (c - i_col))).astype(BF16), state.astype(BF16),
                                   preferred_element_type=F32)
            state = state * chunk_decay + jnp.dot((kc * jnp.exp(lg * j_row)).astype(BF16), vc,
                                                  preferred_element_type=F32)
    o = o_s[...]
    o = o * lax.rsqrt(_head_sum(o * o) * (1.0 / n) + NORM_EPS) * gn_ref[...]
    g = r_ref[0, :, 3 * w:4 * w]
    o_ref[0] = g * jax.nn.sigmoid(g) * o


def _ret(rt, pos, gn_g):
    b, t, cw = rt.shape
    w = GROUP_WIDTH
    return pl.pallas_call(
        _ret_kernel,
        grid=(b,),
        in_specs=[pl.BlockSpec((1, t, cw), lambda i: (i, 0, 0)),
                  pl.BlockSpec((1, t, 1), lambda i: (i, 0, 0)),
                  pl.BlockSpec((1, w), lambda i: (0, 0))],
        out_specs=pl.BlockSpec((1, t, w), lambda i: (i, 0, 0)),
        out_shape=jax.ShapeDtypeStruct((b, t, w), F32),
        scratch_shapes=[pltpu.VMEM((t, w), F32), pltpu.VMEM((w, t), F32), pltpu.VMEM((t, w), BF16),
                        pltpu.VMEM((t, w), F32)],
        compiler_params=_cparams("parallel"),
        name="retention",
    )(rt, pos, gn_g.reshape(1, w))


def _mla_kernel(mq_ref, mkv_ref, pos_ref, qn_ref, qb_ref, kvn_ref, kvb_ref, o_ref, kp_s, v_s, cos_s, sin_s, *, tq):
    w = GROUP_WIDTH
    kw = GROUP_HEADS * LANES
    qi = pl.program_id(1)
    kv0 = MLA_Q_RANK
    half = MLA_ROPE // 2

    @pl.when(qi == 0)
    def _():
        cos, sin = _rope_tables(pos_ref[0], half)
        ckv = _rms(mkv_ref[0, :, kv0:kv0 + MLA_KV_RANK], kvn_ref[...]).astype(BF16)
        kv = jnp.dot(ckv, kvb_ref[...], preferred_element_type=F32)
        v_s[...] = kv[:, kw:kw + w].astype(BF16)
        kr = _apply_rope(mkv_ref[0, :, kv0 + MLA_KV_RANK:MLA_PAD], cos, sin, half)
        kr = pltpu.roll(kr, MLA_NOPE, 1)
        kp_s[...] = (kv[:, 0:kw] + _tile_lanes(kr, kw)).astype(BF16)
        lane = lax.broadcasted_iota(jnp.int32, (1, LANES), 1)
        rope_lane = (lane >= MLA_NOPE) & (lane < MLA_NOPE + MLA_ROPE)
        cos_s[...] = jnp.where(rope_lane, cos, 1.0)
        sin_s[...] = jnp.where(rope_lane, sin, 0.0)

    rows = pl.ds(pl.multiple_of(qi * tq, tq), tq)
    qa = _rms(mq_ref[0, :, 0:MLA_Q_RANK], qn_ref[...]).astype(BF16)
    q = jnp.dot(qa, qb_ref[...], preferred_element_type=F32)
    scale = (MLA_NOPE + MLA_ROPE) ** -0.5 * math.log2(math.e)
    q = (_apply_rope(q, cos_s[rows, :], sin_s[rows, :], half) * scale).astype(BF16)
    nt = (((1,), (1,)), ((), ()))
    outs = []
    for h in range(GROUP_HEADS):
        hl = slice(h * LANES, (h + 1) * LANES)
        s = lax.dot_general(q[:, hl], kp_s[:, hl], nt, preferred_element_type=F32)
        p = jnp.exp2(s - jnp.max(s, axis=-1, keepdims=True))
        den = jnp.sum(p, axis=-1, keepdims=True)
        outs.append(jnp.dot(p.astype(BF16), v_s[:, h * MLA_NOPE:(h + 1) * MLA_NOPE], preferred_element_type=F32) / den)
    o_ref[0] = jnp.concatenate(outs, axis=-1)


def _mla(ml, pos, qn_g, qb, kvn_g, kvb, tq=512):
    b, t, cw = ml.shape
    w = GROUP_WIDTH
    tq = min(tq, t)
    full = lambda a: pl.BlockSpec(a.shape, lambda i, j: (0, 0))
    qn_g = qn_g.reshape(1, -1)
    kvn_g = kvn_g.reshape(1, -1)
    return pl.pallas_call(
        functools.partial(_mla_kernel, tq=tq),
        grid=(b, t // tq),
        in_specs=[pl.BlockSpec((1, tq, cw), lambda i, j: (i, j, 0)),
                  pl.BlockSpec((1, t, cw), lambda i, j: (i, 0, 0)),
                  pl.BlockSpec((1, t, 1), lambda i, j: (i, 0, 0)),
                  full(qn_g), full(qb), full(kvn_g), full(kvb)],
        out_specs=pl.BlockSpec((1, tq, w), lambda i, j: (i, j, 0)),
        out_shape=jax.ShapeDtypeStruct((b, t, w), F32),
        scratch_shapes=[pltpu.VMEM((t, GROUP_HEADS * LANES), BF16), pltpu.VMEM((t, w), BF16),
                        pltpu.VMEM((t, LANES), F32), pltpu.VMEM((t, LANES), F32)],
        compiler_params=_cparams("parallel", "arbitrary"),
        name="mla",
    )(ml, ml, pos, qn_g, qb, kvn_g, kvb)


def _mla_weights(q_b, kv_b):
    qh = q_b.reshape(MLA_Q_RANK, GROUP_HEADS, MLA_NOPE + MLA_ROPE)
    qb = jnp.pad(qh, ((0, 0), (0, 0), (0, LANES - MLA_NOPE - MLA_ROPE))).reshape(MLA_Q_RANK, GROUP_HEADS * LANES)
    kvh = kv_b.reshape(MLA_KV_RANK, GROUP_HEADS, 2 * MLA_NOPE)
    kn = jnp.pad(kvh[:, :, :MLA_NOPE], ((0, 0), (0, 0), (0, LANES - MLA_NOPE))).reshape(MLA_KV_RANK, GROUP_HEADS * LANES)
    v = kvh[:, :, MLA_NOPE:].reshape(MLA_KV_RANK, GROUP_HEADS * MLA_NOPE)
    return qb.astype(BF16), jnp.concatenate([kn, v], axis=1).astype(BF16)


def _outproj_kernel(x_ref, y1, y2, y3, y4, w_ref, o_ref):
    w = GROUP_WIDTH
    acc = x_ref[...]
    for g, y in enumerate((y1, y2, y3, y4)):
        acc = acc + jnp.dot(y[...].astype(BF16), w_ref[g * w:(g + 1) * w, :], preferred_element_type=F32)
    o_ref[...] = acc


def _outproj(x, ys, w_out, tm=1024):
    n, d = x.shape
    w = GROUP_WIDTH
    ytile = pl.BlockSpec((tm, w), lambda i: (i, 0))
    return pl.pallas_call(
        _outproj_kernel,
        grid=(n // tm,),
        in_specs=[pl.BlockSpec((tm, d), lambda i: (i, 0)), ytile, ytile, ytile, ytile,
                  pl.BlockSpec(w_out.shape, lambda i: (0, 0))],
        out_specs=pl.BlockSpec((tm, d), lambda i: (i, 0)),
        out_shape=jax.ShapeDtypeStruct((n, d), F32),
        compiler_params=_cparams("parallel"),
        name="outproj",
    )(x, *ys, w_out)


def kernel(x, positions, ffn1_norm, ffn1_w_gate, ffn1_w_up, ffn1_w_down, mix_norm, w_in, w_out, conv_w, rwkv_w0_f, rwkv_w0_b, rwkv_w2_f, rwkv_w2_b, rwkv_a0_f, rwkv_a0_b, rwkv_a2_f, rwkv_a2_b, rwkv_g2, rwkv_k_k, rwkv_k_a, rwkv_r_k, rwkv_lnx_g, rwkv_lnx_b, ret_gn_g, mla_q_a_norm, mla_q_b, mla_kv_a_norm, mla_kv_b, ffn2_norm, ffn2_w_gate, ffn2_w_up, ffn2_w_down, final_norm):
    b, t, d = x.shape
    n = b * t
    w = GROUP_WIDTH
    depth = w_in.shape[0]
    pos = positions.astype(F32).reshape(b, t, 1)
    xf = x.reshape(n, d)
    for l in range(depth):
        xf = _ffn(xf, ffn1_norm[l], ffn1_w_gate[l].astype(BF16), ffn1_w_up[l].astype(BF16),
                  ffn1_w_down[l].astype(BF16))
        wl = w_in[l].astype(BF16)
        w_mla = jnp.pad(wl[:, MLA_COLS[0]:MLA_COLS[1]], ((0, 0), (0, MLA_PAD - (MLA_COLS[1] - MLA_COLS[0]))))
        ws = [wl[:, CONV_COLS[0]:CONV_COLS[1]], wl[:, RWKV_COLS[0]:RWKV_COLS[1]], wl[:, RET_COLS[0]:RET_COLS[1]], w_mla]
        cv, rw, rt, ml = _inproj(xf, mix_norm[l], ws)
        y_conv = _conv(cv.reshape(b, t, -1), conv_w[l])
        rp = dict(w0_f=rwkv_w0_f[l], w0_b=rwkv_w0_b[l], w2_f=rwkv_w2_f[l], w2_b=rwkv_w2_b[l],
                  a0_f=rwkv_a0_f[l], a0_b=rwkv_a0_b[l], a2_f=rwkv_a2_f[l], a2_b=rwkv_a2_b[l],
                  g2=rwkv_g2[l], k_k=rwkv_k_k[l], k_a=rwkv_k_a[l], r_k=rwkv_r_k[l])
        pf, pb, g, bonus = _rwkv_prep(rw, rp)
        yf, yb = _rwkv_scan(pf.reshape(b, t, -1), pb.reshape(b, t, -1))
        y_rwkv = _rwkv_post(yf, yb, g, bonus, rwkv_lnx_g[l], rwkv_lnx_b[l]).reshape(n, w)
        y_ret = _ret(rt.reshape(b, t, -1), pos, ret_gn_g[l])
        qb, kvb = _mla_weights(mla_q_b[l], mla_kv_b[l])
        y_mla = _mla(ml.reshape(b, t, -1), pos, mla_q_a_norm[l], qb, mla_kv_a_norm[l], kvb)
        xf = _outproj(xf, (y_conv.reshape(n, w), y_rwkv, y_ret.reshape(n, w), y_mla.reshape(n, w)),
                      w_out[l].astype(BF16))
        xf = _ffn(xf, ffn2_norm[l], ffn2_w_gate[l].astype(BF16), ffn2_w_up[l].astype(BF16),
                  ffn2_w_down[l].astype(BF16), final_g=final_norm if l == depth - 1 else None)
    return xf.reshape(b, t, d)
```

```python
import functools
import math

import jax
import jax.numpy as jnp
from jax import lax
from jax.experimental import pallas as pl
from jax.experimental.pallas import tpu as pltpu

F32 = jnp.float32
BF16 = jnp.bfloat16

D_MODEL = 1024
GROUP_HEADS = 4
GROUP_WIDTH = 256
HEAD_DIM = 64
D_FF = 2816
RWKV_GN_EPS = 64e-5
MLA_Q_RANK = 384
MLA_KV_RANK = 128
MLA_NOPE = 64
MLA_ROPE = 32
ROPE_BASE = 10000.0
NORM_EPS = 1e-6
RET_LOG_GAMMA = tuple(math.log(1.0 - 2.0 ** (-5.0 - h)) for h in range(GROUP_HEADS))
RET_CHUNK = 128

CONV_COLS = (0, 768)
RWKV_COLS = (768, 1920)
RET_COLS = (1920, 2944)
MLA_COLS = (2944, 3488)
MLA_PAD = 640

VMEM_LIMIT = 56 * 1024 * 1024
LANES = 128
MXU_TILE = 256


def _cparams(*sem):
    return pltpu.CompilerParams(dimension_semantics=sem, vmem_limit_bytes=VMEM_LIMIT)


def _rms(x, g):
    return x * lax.rsqrt(jnp.mean(x * x, axis=-1, keepdims=True) + NORM_EPS) * g


def _head_ones(width, head):
    r = lax.broadcasted_iota(jnp.int32, (width, width), 0) // head
    c = lax.broadcasted_iota(jnp.int32, (width, width), 1) // head
    return r == c


def _split_bf16(x):
    hi = x.astype(BF16)
    return hi, (x - hi.astype(F32)).astype(BF16)


def _head_sum(x, head=HEAD_DIM):
    g = _head_ones(x.shape[-1], head).astype(BF16)
    hi, lo = _split_bf16(x)
    return jnp.dot(hi, g, preferred_element_type=F32) + jnp.dot(lo, g, preferred_element_type=F32)


def _rotate_half(x, half):
    n = x.shape[-1]
    lane = lax.broadcasted_iota(jnp.int32, x.shape, x.ndim - 1)
    first = (lane % (2 * half)) < half
    from_right = pltpu.roll(x, n - half, x.ndim - 1)
    from_left = pltpu.roll(x, half, x.ndim - 1)
    return jnp.where(first, -from_right, from_left)


def _rope_tables(pos, half):
    lane = lax.broadcasted_iota(jnp.int32, (1, LANES), 1)
    inv = jnp.exp((lane % half).astype(F32) * (-math.log(ROPE_BASE) / half))
    ang = pos * inv
    return jnp.cos(ang), jnp.sin(ang)


def _tile_lanes(x, width):
    return x if width == x.shape[-1] else jnp.concatenate([x] * (width // x.shape[-1]), axis=-1)


def _apply_rope(x, cos, sin, half):
    width = x.shape[-1]
    return x * _tile_lanes(cos, width) + _rotate_half(x, half) * _tile_lanes(sin, width)


def _ffn_kernel(x_ref, g_ref, wg_ref, wu_ref, wd_ref, *rest, chunks, final):
    if final:
        fg_ref, o_ref = rest
    else:
        (o_ref,) = rest
    x = x_ref[...]
    h = _rms(x, g_ref[...]).astype(BF16)
    acc = None
    f0 = 0
    for fc in chunks:
        cols = slice(f0, f0 + fc)
        gate = jnp.dot(h, wg_ref[:, cols], preferred_element_type=F32)
        up = jnp.dot(h, wu_ref[:, cols], preferred_element_type=F32)
        act = (gate * jax.nn.sigmoid(gate) * up).astype(BF16)
        part = jnp.dot(act, wd_ref[cols, :], preferred_element_type=F32)
        acc = part if acc is None else acc + part
        f0 += fc
    y = x + 0.5 * acc
    if final:
        y = _rms(y, fg_ref[...])
    o_ref[...] = y


def _ffn(x, g, wg, wu, wd, final_g=None, tm=512, max_chunk=6 * MXU_TILE):
    n, d = x.shape
    f = wg.shape[1]
    chunks = [max_chunk] * (f // max_chunk) + ([f % max_chunk] if f % max_chunk else [])
    final = final_g is not None
    resident = lambda shape: pl.BlockSpec(shape, lambda i: (0, 0), pipeline_mode=pl.Buffered(1))
    in_specs = [
        pl.BlockSpec((tm, d), lambda i: (i, 0)),
        pl.BlockSpec((1, d), lambda i: (0, 0)),
        resident((d, f)), resident((d, f)), resident((f, d)),
    ]
    args = [x, g.reshape(1, d), wg, wu, wd]
    if final:
        in_specs.append(pl.BlockSpec((1, d), lambda i: (0, 0)))
        args.append(final_g.reshape(1, d))
    return pl.pallas_call(
        functools.partial(_ffn_kernel, chunks=tuple(chunks), final=final),
        grid=(n // tm,),
        in_specs=in_specs,
        out_specs=pl.BlockSpec((tm, d), lambda i: (i, 0)),
        out_shape=jax.ShapeDtypeStruct((n, d), F32),
        compiler_params=_cparams("parallel"),
        name="ffn",
    )(*args)


def _inproj_kernel(x_ref, g_ref, w1, w2, w3, w4, o1, o2, o3, o4):
    h = _rms(x_ref[...], g_ref[...]).astype(BF16)
    for w, o in ((w1, o1), (w2, o2), (w3, o3), (w4, o4)):
        o[...] = jnp.dot(h, w[...], preferred_element_type=F32).astype(o.dtype)


def _inproj(x, g, ws, tm=512):
    n, d = x.shape
    widths = [w.shape[1] for w in ws]
    return pl.pallas_call(
        _inproj_kernel,
        grid=(n // tm,),
        in_specs=[pl.BlockSpec((tm, d), lambda i: (i, 0)), pl.BlockSpec((1, d), lambda i: (0, 0))]
        + [pl.BlockSpec((d, wd), lambda i: (0, 0)) for wd in widths],
        out_specs=[pl.BlockSpec((tm, wd), lambda i: (i, 0)) for wd in widths],
        out_shape=[jax.ShapeDtypeStruct((n, wd), BF16) for wd in widths],
        compiler_params=_cparams("parallel"),
        name="inproj",
    )(x, g.reshape(1, d), *ws)


def _conv_kernel(c_ref, w_ref, o_ref):
    t = c_ref.shape[1]
    w = GROUP_WIDTH
    u = c_ref[0, :, 2 * w:3 * w].astype(F32) * c_ref[0, :, 0:w].astype(F32)
    row = lax.broadcasted_iota(jnp.int32, u.shape, 0)
    prev = jnp.where(row == 0, 0.0, pltpu.roll(u, 1, 0))
    nxt = jnp.where(row == t - 1, 0.0, pltpu.roll(u, t - 1, 0))
    y = prev * w_ref[0:1, :] + u * w_ref[1:2, :] + nxt * w_ref[2:3, :]
    o_ref[0] = c_ref[0, :, w:2 * w].astype(F32) * y


def _conv(c, conv_w):
    b, t, cw = c.shape
    return pl.pallas_call(
        _conv_kernel,
        grid=(b,),
        in_specs=[pl.BlockSpec((1, t, cw), lambda i: (i, 0, 0)), pl.BlockSpec((3, GROUP_WIDTH), lambda i: (0, 0))],
        out_specs=pl.BlockSpec((1, t, GROUP_WIDTH), lambda i: (i, 0, 0)),
        out_shape=jax.ShapeDtypeStruct((b, t, GROUP_WIDTH), F32),
        compiler_params=_cparams("parallel"),
        name="conv",
    )(c, conv_w)


def _softplus(z):
    return jnp.maximum(z, 0.0) + jnp.log(1.0 + jnp.exp(-jnp.abs(z)))


def _hdot(a, b):
    ah, al = _split_bf16(a)
    bh, bl = _split_bf16(b)
    dot = functools.partial(jnp.dot, preferred_element_type=F32)
    return dot(ah, bh) + (dot(ah, bl) + dot(al, bh))


def _rwkv_prep_kernel(rw_ref, w0f, w0b, w2f, w2b, a0f, a0b, a2f, a2b, g2, kk_w, ka_w, rk_w,
                      pf_ref, pb_ref, g_ref, bonus_ref):
    w = GROUP_WIDTH

    def cols(lo, hi):
        return rw_ref[:, lo:hi].astype(F32)

    r, k, v = cols(0, w), cols(w, 2 * w), cols(2 * w, 3 * w)
    base = 3 * w
    wd = (cols(base, base + 64), cols(base + 64, base + 128))
    ad = (cols(base + 128, base + 192), cols(base + 192, base + 256))
    gd = cols(base + 256, base + 384)
    g_ref[...] = _hdot(jax.nn.sigmoid(gd), g2[...])
    kk = k * kk_w[...]
    kk = kk / jnp.maximum(jnp.sqrt(_head_sum(kk * kk)), 1e-12)
    k_sum = jnp.zeros_like(k)
    for d, (w0, w2, a0, a2, p_ref) in enumerate(((w0f, w2f, a0f, a2f, pf_ref), (w0b, w2b, a0b, a2b, pb_ref))):
        w_log = -_softplus(-(w0[...] + _hdot(jnp.tanh(wd[d]), w2[...]))) - 0.5
        decay = jnp.exp(-jnp.exp(w_log))
        a = jax.nn.sigmoid(a0[...] + _hdot(ad[d], a2[...]))
        k_dir = k * (1.0 + (a - 1.0) * ka_w[...])
        k_sum = k_sum + k_dir
        p_ref[:, 0:w] = r
        p_ref[:, w:2 * w] = kk
        p_ref[:, 2 * w:3 * w] = v
        p_ref[:, 3 * w:4 * w] = decay
        p_ref[:, 4 * w:5 * w] = k_dir
        p_ref[:, 5 * w:6 * w] = kk * a
    bonus_ref[...] = _head_sum(r * k_sum * rk_w[...]) * v


def _rwkv_prep(rw, p, tm=512):
    n, cw = rw.shape
    w = GROUP_WIDTH
    row = lambda a: a.reshape(1, w)
    params = [row(p["w0_f"]), row(p["w0_b"]), p["w2_f"], p["w2_b"], row(p["a0_f"]), row(p["a0_b"]),
              p["a2_f"], p["a2_b"], p["g2"], row(p["k_k"]), row(p["k_a"]), row(p["r_k"])]
    full = lambda a: pl.BlockSpec(a.shape, lambda i: (0, 0))
    return pl.pallas_call(
        _rwkv_prep_kernel,
        grid=(n // tm,),
        in_specs=[pl.BlockSpec((tm, cw), lambda i: (i, 0))] + [full(a) for a in params],
        out_specs=[pl.BlockSpec((tm, 6 * w), lambda i: (i, 0)), pl.BlockSpec((tm, 6 * w), lambda i: (i, 0)),
                   pl.BlockSpec((tm, w), lambda i: (i, 0)), pl.BlockSpec((tm, w), lambda i: (i, 0))],
        out_shape=[jax.ShapeDtypeStruct((n, 6 * w), F32), jax.ShapeDtypeStruct((n, 6 * w), F32),
                   jax.ShapeDtypeStruct((n, w), F32), jax.ShapeDtypeStruct((n, w), F32)],
        compiler_params=_cparams("parallel"),
        name="rwkv_prep",
    )(rw, *params)


SLAB = HEAD_DIM
SCAN_GROUP = 4


def _rwkv_scan_kernel(pf_ref, pb_ref, yf_ref, yb_ref, s_ref, q_ref, *, nb):
    w = GROUP_WIDTH
    n = HEAD_DIM

    @pl.when(pl.program_id(0) == 0)
    def _():
        s_ref[...] = jnp.zeros_like(s_ref)
        q_ref[...] = jnp.zeros_like(q_ref)

    ones = _head_ones(w, n).astype(BF16)
    slot = lax.broadcasted_iota(jnp.int32, (n, w), 1) % n
    diag = (lax.broadcasted_iota(jnp.int32, (n, w), 0) == slot).astype(BF16)

    def group_step(chains):
        def row(ch, col):
            return ch[1][ch[3], pl.ds(ch[4], 1), col * w:(col + 1) * w]

        pieces = []
        for ch in chains:
            pieces += [s_ref[ch[0]].astype(BF16) * row(ch, 1).astype(BF16), diag * row(ch, 2).astype(BF16),
                       q_ref[ch[0]]]
        res = jnp.dot(jnp.concatenate(pieces, axis=0), ones, preferred_element_type=F32)
        for i, ch in enumerate(chains):
            c, _, y_ref, b, _, mask = ch
            u, vcol, y_prev = (res[(3 * i + k) * n:(3 * i + k + 1) * n] for k in range(3))
            s = s_ref[c] * row(ch, 3) - u * row(ch, 5) + vcol * row(ch, 4)
            s_ref[c] = s
            q_ref[c] = s.astype(BF16) * row(ch, 0).astype(BF16)
            pltpu.store(y_ref.at[b, 0], y_prev, mask=mask)

    def step(j, carry):
        mask_f = slot == j - 1
        mask_b = slot == SLAB - j
        chains = [(b, pf_ref, yf_ref, b, j, mask_f) for b in range(nb)]
        chains += [(nb + b, pb_ref, yb_ref, b, SLAB - 1 - j, mask_b) for b in range(nb)]
        for g in range(0, 2 * nb, SCAN_GROUP):
            group_step(chains[g:g + SCAN_GROUP])
        return carry

    lax.fori_loop(0, SLAB, step, 0, unroll=4)
    for b in range(nb):
        for c, y_ref, last in ((b, yf_ref, SLAB - 1), (nb + b, yb_ref, 0)):
            y_last = jnp.dot(q_ref[c], ones, preferred_element_type=F32)
            pltpu.store(y_ref.at[b, 0], y_last, mask=slot == last)


def _rwkv_scan(pf, pb):
    b, t, cw = pf.shape
    w = GROUP_WIDTH
    nt = t // SLAB
    out = jax.ShapeDtypeStruct((b, nt, HEAD_DIM, w), F32)
    return pl.pallas_call(
        functools.partial(_rwkv_scan_kernel, nb=b),
        grid=(nt,),
        in_specs=[pl.BlockSpec((b, SLAB, cw), lambda j: (0, j, 0)),
                  pl.BlockSpec((b, SLAB, cw), lambda j: (0, nt - 1 - j, 0))],
        out_specs=[pl.BlockSpec((b, 1, HEAD_DIM, w), lambda j: (0, j, 0, 0)),
                   pl.BlockSpec((b, 1, HEAD_DIM, w), lambda j: (0, nt - 1 - j, 0, 0))],
        out_shape=[out, out],
        scratch_shapes=[pltpu.VMEM((2 * b, HEAD_DIM, w), F32), pltpu.VMEM((2 * b, HEAD_DIM, w), BF16)],
        compiler_params=_cparams("arbitrary"),
        name="rwkv_scan",
    )(pf, pb)


def _rwkv_finish(yf_ref, yb_ref, g_ref, bonus_ref, lg_ref, lb_ref, o_ref):
    n = HEAD_DIM
    for pair in range(yf_ref.shape[1] // 2):
        zs = []
        for s in (2 * pair, 2 * pair + 1):
            y = yf_ref[0, s] + yb_ref[0, s]
            d = y - jnp.mean(y, axis=0, keepdims=True)
            var = jnp.mean(d * d, axis=0, keepdims=True)
            zs.append(d * lax.rsqrt(var + RWKV_GN_EPS))
        z = jnp.concatenate(zs, axis=0)
        zt = [z[:, 0:2 * n].T, z[:, 2 * n:4 * n].T]
        for i, s in enumerate((2 * pair, 2 * pair + 1)):
            yn = jnp.concatenate([zt[h // 2][(h % 2) * n:(h % 2 + 1) * n, i * n:(i + 1) * n]
                                  for h in range(GROUP_HEADS)], axis=1)
            rows = slice(s * n, (s + 1) * n)
            o_ref[rows, :] = (yn * lg_ref[...] + lb_ref[...] + bonus_ref[rows, :]) * g_ref[rows, :]


def _ret_kernel(r_ref, pos_ref, gn_ref, o_ref, q_s, kt_s, v_s, o_s):
    w = GROUP_WIDTH
    n = HEAD_DIM
    c = RET_CHUNK
    nc = r_ref.shape[1] // c
    cos, sin = _rope_tables(pos_ref[0], n // 2)
    q_s[...] = _apply_rope(r_ref[0, :, 0:w].astype(F32), cos, sin, n // 2) * (n ** -0.5)
    kt_s[...] = _apply_rope(r_ref[0, :, w:2 * w].astype(F32), cos, sin, n // 2).T
    v_s[...] = r_ref[0, :, 2 * w:3 * w].astype(BF16)
    dist = jnp.abs(lax.broadcasted_iota(jnp.int32, (c, c), 0) - lax.broadcasted_iota(jnp.int32, (c, c), 1)).astype(F32)
    i_col = lax.broadcasted_iota(jnp.int32, (c, n), 0).astype(F32)
    j_row = lax.broadcasted_iota(jnp.int32, (n, c), 1).astype(F32)
    for h in range(GROUP_HEADS):
        lg = RET_LOG_GAMMA[h]
        hs = slice(h * n, (h + 1) * n)
        chunk_decay = math.exp(lg * c)
        gam = jnp.exp(lg * dist)
        state = jnp.zeros((n, n), F32)
        for ch in range(nc):
            ts = slice(ch * c, (ch + 1) * c)
            qc, kc, vc = q_s[ts, hs], kt_s[hs, ts], v_s[ts, hs]
            s = jnp.dot(qc.astype(BF16), kc.astype(BF16), preferred_element_type=F32) * gam
            o = jnp.dot(s.astype(BF16), vc, preferred_element_type=F32)
            o_s[ts, hs] = o + jnp.dot((qc * jnp.exp(lg * i_col)).astype(BF16), state.astype(BF16),
                                      preferred_element_type=F32)
            state = state * chunk_decay + jnp.dot((kc * jnp.exp(lg * (c - j_row))).astype(BF16), vc,
                                                  preferred_element_type=F32)
        state = jnp.zeros((n, n), F32)
        for ch in reversed(range(nc)):
            ts = slice(ch * c, (ch + 1) * c)
            qc, kc, vc = q_s[ts, hs], kt_s[hs, ts], v_s[ts, hs]
            o_s[ts, hs] += jnp.dot((qc * jnp.exp(lg * (c - i_col))).astype(BF16), state.astype(BF16),
                                   preferred_element_type=F32)
            state = state * chunk_decay + jnp.dot((kc * jnp.exp(lg * j_row)).astype(BF16), vc,
                                                  preferred_element_type=F32)
    o = o_s[...]
    o = o * lax.rsqrt(_head_sum(o * o) * (1.0 / n) + NORM_EPS) * gn_ref[...]
    g = r_ref[0, :, 3 * w:4 * w].astype(F32)
    o_ref[0] = g * jax.nn.sigmoid(g) * o


def _ret(rt, pos, gn_g):
    b, t, cw = rt.shape
    w = GROUP_WIDTH
    return pl.pallas_call(
        _ret_kernel,
        grid=(b,),
        in_specs=[pl.BlockSpec((1, t, cw), lambda i: (i, 0, 0)),
                  pl.BlockSpec((1, t, 1), lambda i: (i, 0, 0)),
                  pl.BlockSpec((1, w), lambda i: (0, 0))],
        out_specs=pl.BlockSpec((1, t, w), lambda i: (i, 0, 0)),
        out_shape=jax.ShapeDtypeStruct((b, t, w), F32),
        scratch_shapes=[pltpu.VMEM((t, w), F32), pltpu.VMEM((w, t), F32), pltpu.VMEM((t, w), BF16),
                        pltpu.VMEM((t, w), F32)],
        compiler_params=_cparams("parallel"),
        name="retention",
    )(rt, pos, gn_g.reshape(1, w))


def _mla_kernel(mq_ref, mkv_ref, pos_ref, qn_ref, qb_ref, kvn_ref, kvb_ref, o_ref, kp_s, v_s, cos_s, sin_s, *, tq):
    w = GROUP_WIDTH
    kw = GROUP_HEADS * LANES
    qi = pl.program_id(1)
    kv0 = MLA_Q_RANK
    half = MLA_ROPE // 2

    @pl.when(qi == 0)
    def _():
        cos, sin = _rope_tables(pos_ref[0], half)
        ckv = _rms(mkv_ref[0, :, kv0:kv0 + MLA_KV_RANK].astype(F32), kvn_ref[...]).astype(BF16)
        kv = jnp.dot(ckv, kvb_ref[...], preferred_element_type=F32)
        v_s[...] = kv[:, kw:kw + w].astype(BF16)
        kr = _apply_rope(mkv_ref[0, :, kv0 + MLA_KV_RANK:MLA_PAD].astype(F32), cos, sin, half)
        kr = pltpu.roll(kr, MLA_NOPE, 1)
        kp_s[...] = (kv[:, 0:kw] + _tile_lanes(kr, kw)).astype(BF16)
        lane = lax.broadcasted_iota(jnp.int32, (1, LANES), 1)
        rope_lane = (lane >= MLA_NOPE) & (lane < MLA_NOPE + MLA_ROPE)
        cos_s[...] = jnp.where(rope_lane, cos, 1.0)
        sin_s[...] = jnp.where(rope_lane, sin, 0.0)

    rows = pl.ds(pl.multiple_of(qi * tq, tq), tq)
    qa = _rms(mq_ref[0, :, 0:MLA_Q_RANK].astype(F32), qn_ref[...]).astype(BF16)
    q = jnp.dot(qa, qb_ref[...], preferred_element_type=F32)
    scale = (MLA_NOPE + MLA_ROPE) ** -0.5 * math.log2(math.e)
    q = (_apply_rope(q, cos_s[rows, :], sin_s[rows, :], half) * scale).astype(BF16)
    nt = (((1,), (1,)), ((), ()))
    outs = []
    for h in range(GROUP_HEADS):
        hl = slice(h * LANES, (h + 1) * LANES)
        s = lax.dot_general(q[:, hl], kp_s[:, hl], nt, preferred_element_type=F32)
        p = jnp.exp2(s - jnp.max(s, axis=-1, keepdims=True))
        den = jnp.sum(p, axis=-1, keepdims=True)
        outs.append(jnp.dot(p.astype(BF16), v_s[:, h * MLA_NOPE:(h + 1) * MLA_NOPE], preferred_element_type=F32) / den)
    o_ref[0] = jnp.concatenate(outs, axis=-1)


def _mla(ml, pos, qn_g, qb, kvn_g, kvb, tq=512):
    b, t, cw = ml.shape
    w = GROUP_WIDTH
    tq = min(tq, t)
    full = lambda a: pl.BlockSpec(a.shape, lambda i, j: (0, 0))
    qn_g = qn_g.reshape(1, -1)
    kvn_g = kvn_g.reshape(1, -1)
    return pl.pallas_call(
        functools.partial(_mla_kernel, tq=tq),
        grid=(b, t // tq),
        in_specs=[pl.BlockSpec((1, tq, cw), lambda i, j: (i, j, 0)),
                  pl.BlockSpec((1, t, cw), lambda i, j: (i, 0, 0)),
                  pl.BlockSpec((1, t, 1), lambda i, j: (i, 0, 0)),
                  full(qn_g), full(qb), full(kvn_g), full(kvb)],
        out_specs=pl.BlockSpec((1, tq, w), lambda i, j: (i, j, 0)),
        out_shape=jax.ShapeDtypeStruct((b, t, w), F32),
        scratch_shapes=[pltpu.VMEM((t, GROUP_HEADS * LANES), BF16), pltpu.VMEM((t, w), BF16),
                        pltpu.VMEM((t, LANES), F32), pltpu.VMEM((t, LANES), F32)],
        compiler_params=_cparams("parallel", "arbitrary"),
        name="mla",
    )(ml, ml, pos, qn_g, qb, kvn_g, kvb)


def _mla_weights(q_b, kv_b):
    qh = q_b.reshape(MLA_Q_RANK, GROUP_HEADS, MLA_NOPE + MLA_ROPE)
    qb = jnp.pad(qh, ((0, 0), (0, 0), (0, LANES - MLA_NOPE - MLA_ROPE))).reshape(MLA_Q_RANK, GROUP_HEADS * LANES)
    kvh = kv_b.reshape(MLA_KV_RANK, GROUP_HEADS, 2 * MLA_NOPE)
    kn = jnp.pad(kvh[:, :, :MLA_NOPE], ((0, 0), (0, 0), (0, LANES - MLA_NOPE))).reshape(MLA_KV_RANK, GROUP_HEADS * LANES)
    v = kvh[:, :, MLA_NOPE:].reshape(MLA_KV_RANK, GROUP_HEADS * MLA_NOPE)
    return qb.astype(BF16), jnp.concatenate([kn, v], axis=1).astype(BF16)


def _outproj_kernel(x_ref, yc_ref, yf_ref, yb_ref, g_ref, bonus_ref, lg_ref, lb_ref, yr_ref, ym_ref, w_ref, o_ref,
                    yw_s):
    w = GROUP_WIDTH
    _rwkv_finish(yf_ref, yb_ref, g_ref, bonus_ref, lg_ref, lb_ref, yw_s)
    acc = x_ref[...]
    for g, y in enumerate((yc_ref, yw_s, yr_ref, ym_ref)):
        acc = acc + jnp.dot(y[...].astype(BF16), w_ref[g * w:(g + 1) * w, :], preferred_element_type=F32)
    o_ref[...] = acc


def _outproj(x, y_conv, rwkv, y_ret, y_mla, w_out, tm=1024):
    yf, yb, g, bonus, lnx_g, lnx_b = rwkv
    n, d = x.shape
    w = GROUP_WIDTH
    b, nt, hd, _ = yf.shape
    tm = min(tm, nt * hd)
    per_seq = nt * hd // tm
    ytile = pl.BlockSpec((tm, w), lambda i: (i, 0))
    slabs = pl.BlockSpec((1, tm // hd, hd, w), lambda i: (i // per_seq, i % per_seq, 0, 0))
    vec = pl.BlockSpec((1, w), lambda i: (0, 0))
    return pl.pallas_call(
        _outproj_kernel,
        grid=(n // tm,),
        in_specs=[pl.BlockSpec((tm, d), lambda i: (i, 0)), ytile, slabs, slabs, ytile, ytile, vec, vec, ytile, ytile,
                  pl.BlockSpec(w_out.shape, lambda i: (0, 0))],
        out_specs=pl.BlockSpec((tm, d), lambda i: (i, 0)),
        out_shape=jax.ShapeDtypeStruct((n, d), F32),
        scratch_shapes=[pltpu.VMEM((tm, w), F32)],
        compiler_params=_cparams("parallel"),
        name="outproj",
    )(x, y_conv, yf, yb, g, bonus, lnx_g.reshape(1, w), lnx_b.reshape(1, w), y_ret, y_mla, w_out)


def kernel(x, positions, ffn1_norm, ffn1_w_gate, ffn1_w_up, ffn1_w_down, mix_norm, w_in, w_out, conv_w, rwkv_w0_f, rwkv_w0_b, rwkv_w2_f, rwkv_w2_b, rwkv_a0_f, rwkv_a0_b, rwkv_a2_f, rwkv_a2_b, rwkv_g2, rwkv_k_k, rwkv_k_a, rwkv_r_k, rwkv_lnx_g, rwkv_lnx_b, ret_gn_g, mla_q_a_norm, mla_q_b, mla_kv_a_norm, mla_kv_b, ffn2_norm, ffn2_w_gate, ffn2_w_up, ffn2_w_down, final_norm):
    b, t, d = x.shape
    n = b * t
    w = GROUP_WIDTH
    depth = w_in.shape[0]
    pos = positions.astype(F32).reshape(b, t, 1)
    xf = x.reshape(n, d)
    for l in range(depth):
        xf = _ffn(xf, ffn1_norm[l], ffn1_w_gate[l].astype(BF16), ffn1_w_up[l].astype(BF16),
                  ffn1_w_down[l].astype(BF16))
        wl = w_in[l].astype(BF16)
        w_mla = jnp.pad(wl[:, MLA_COLS[0]:MLA_COLS[1]], ((0, 0), (0, MLA_PAD - (MLA_COLS[1] - MLA_COLS[0]))))
        ws = [wl[:, CONV_COLS[0]:CONV_COLS[1]], wl[:, RWKV_COLS[0]:RWKV_COLS[1]], wl[:, RET_COLS[0]:RET_COLS[1]], w_mla]
        cv, rw, rt, ml = _inproj(xf, mix_norm[l], ws)
        y_conv = _conv(cv.reshape(b, t, -1), conv_w[l])
        rp = dict(w0_f=rwkv_w0_f[l], w0_b=rwkv_w0_b[l], w2_f=rwkv_w2_f[l], w2_b=rwkv_w2_b[l],
                  a0_f=rwkv_a0_f[l], a0_b=rwkv_a0_b[l], a2_f=rwkv_a2_f[l], a2_b=rwkv_a2_b[l],
                  g2=rwkv_g2[l], k_k=rwkv_k_k[l], k_a=rwkv_k_a[l], r_k=rwkv_r_k[l])
        pf, pb, g, bonus = _rwkv_prep(rw, rp)
        yf, yb = _rwkv_scan(pf.reshape(b, t, -1), pb.reshape(b, t, -1))
        y_ret = _ret(rt.reshape(b, t, -1), pos, ret_gn_g[l])
        qb, kvb = _mla_weights(mla_q_b[l], mla_kv_b[l])
        y_mla = _mla(ml.reshape(b, t, -1), pos, mla_q_a_norm[l], qb, mla_kv_a_norm[l], kvb)
        xf = _outproj(xf, y_conv.reshape(n, w), (yf, yb, g, bonus, rwkv_lnx_g[l], rwkv_lnx_b[l]),
                      y_ret.reshape(n, w), y_mla.reshape(n, w), w_out[l].astype(BF16))
        xf = _ffn(xf, ffn2_norm[l], ffn2_w_gate[l].astype(BF16), ffn2_w_up[l].astype(BF16),
                  ffn2_w_down[l].astype(BF16), final_g=final_norm if l == depth - 1 else None)
    return xf.reshape(b, t, d)
```

```python
import functools
import math

import jax
import jax.numpy as jnp
from jax import lax
from jax.experimental import pallas as pl
from jax.experimental.pallas import tpu as pltpu

F32 = jnp.float32
BF16 = jnp.bfloat16

D_MODEL = 1024
GROUP_HEADS = 4
GROUP_WIDTH = 256
HEAD_DIM = 64
D_FF = 2816
RWKV_GN_EPS = 64e-5
MLA_Q_RANK = 384
MLA_KV_RANK = 128
MLA_NOPE = 64
MLA_ROPE = 32
ROPE_BASE = 10000.0
NORM_EPS = 1e-6
RET_LOG_GAMMA = tuple(math.log(1.0 - 2.0 ** (-5.0 - h)) for h in range(GROUP_HEADS))
RET_CHUNK = 128

CONV_COLS = (0, 768)
RWKV_COLS = (768, 1920)
RET_COLS = (1920, 2944)
MLA_COLS = (2944, 3488)
MLA_PAD = 640

VMEM_LIMIT = 56 * 1024 * 1024
LANES = 128
MXU_TILE = 256


def _cparams(*sem):
    return pltpu.CompilerParams(dimension_semantics=sem, vmem_limit_bytes=VMEM_LIMIT)


def _rms(x, g):
    return x * lax.rsqrt(jnp.mean(x * x, axis=-1, keepdims=True) + NORM_EPS) * g


def _head_ones(width, head):
    r = lax.broadcasted_iota(jnp.int32, (width, width), 0) // head
    c = lax.broadcasted_iota(jnp.int32, (width, width), 1) // head
    return r == c


def _split_bf16(x):
    hi = x.astype(BF16)
    return hi, (x - hi.astype(F32)).astype(BF16)


def _head_sum(x, head=HEAD_DIM):
    g = _head_ones(x.shape[-1], head).astype(BF16)
    hi, lo = _split_bf16(x)
    return jnp.dot(hi, g, preferred_element_type=F32) + jnp.dot(lo, g, preferred_element_type=F32)


def _rotate_half(x, half):
    n = x.shape[-1]
    lane = lax.broadcasted_iota(jnp.int32, x.shape, x.ndim - 1)
    first = (lane % (2 * half)) < half
    from_right = pltpu.roll(x, n - half, x.ndim - 1)
    from_left = pltpu.roll(x, half, x.ndim - 1)
    return jnp.where(first, -from_right, from_left)


def _rope_tables(pos, half):
    lane = lax.broadcasted_iota(jnp.int32, (1, LANES), 1)
    inv = jnp.exp((lane % half).astype(F32) * (-math.log(ROPE_BASE) / half))
    ang = pos * inv
    return jnp.cos(ang), jnp.sin(ang)


def _tile_lanes(x, width):
    return x if width == x.shape[-1] else jnp.concatenate([x] * (width // x.shape[-1]), axis=-1)


def _apply_rope(x, cos, sin, half):
    width = x.shape[-1]
    return x * _tile_lanes(cos, width) + _rotate_half(x, half) * _tile_lanes(sin, width)


def _ffn_kernel(x_ref, g_ref, wg_ref, wu_ref, wd_ref, *rest, chunks, final):
    if final:
        fg_ref, o_ref = rest
    else:
        (o_ref,) = rest
    x = x_ref[...]
    h = _rms(x, g_ref[...]).astype(BF16)
    acc = None
    f0 = 0
    for fc in chunks:
        cols = slice(f0, f0 + fc)
        gate = jnp.dot(h, wg_ref[:, cols], preferred_element_type=F32)
        up = jnp.dot(h, wu_ref[:, cols], preferred_element_type=F32)
        act = (gate * jax.nn.sigmoid(gate) * up).astype(BF16)
        part = jnp.dot(act, wd_ref[cols, :], preferred_element_type=F32)
        acc = part if acc is None else acc + part
        f0 += fc
    y = x + 0.5 * acc
    if final:
        y = _rms(y, fg_ref[...])
    o_ref[...] = y


def _ffn(x, g, wg, wu, wd, final_g=None, tm=512, max_chunk=6 * MXU_TILE):
    n, d = x.shape
    f = wg.shape[1]
    chunks = [max_chunk] * (f // max_chunk) + ([f % max_chunk] if f % max_chunk else [])
    final = final_g is not None
    resident = lambda shape: pl.BlockSpec(shape, lambda i: (0, 0), pipeline_mode=pl.Buffered(1))
    in_specs = [
        pl.BlockSpec((tm, d), lambda i: (i, 0)),
        pl.BlockSpec((1, d), lambda i: (0, 0)),
        resident((d, f)), resident((d, f)), resident((f, d)),
    ]
    args = [x, g.reshape(1, d), wg, wu, wd]
    if final:
        in_specs.append(pl.BlockSpec((1, d), lambda i: (0, 0)))
        args.append(final_g.reshape(1, d))
    return pl.pallas_call(
        functools.partial(_ffn_kernel, chunks=tuple(chunks), final=final),
        grid=(n // tm,),
        in_specs=in_specs,
        out_specs=pl.BlockSpec((tm, d), lambda i: (i, 0)),
        out_shape=jax.ShapeDtypeStruct((n, d), F32),
        compiler_params=_cparams("parallel"),
        name="ffn",
    )(*args)


def _inproj_kernel(x_ref, g_ref, w1, w2, w3, w4, o1, o2, o3, o4):
    h = _rms(x_ref[...], g_ref[...]).astype(BF16)
    for w, o in ((w1, o1), (w2, o2), (w3, o3), (w4, o4)):
        o[...] = jnp.dot(h, w[...], preferred_element_type=F32).astype(o.dtype)


def _inproj(x, g, ws, tm=512):
    n, d = x.shape
    widths = [w.shape[1] for w in ws]
    return pl.pallas_call(
        _inproj_kernel,
        grid=(n // tm,),
        in_specs=[pl.BlockSpec((tm, d), lambda i: (i, 0)), pl.BlockSpec((1, d), lambda i: (0, 0))]
        + [pl.BlockSpec((d, wd), lambda i: (0, 0)) for wd in widths],
        out_specs=[pl.BlockSpec((tm, wd), lambda i: (i, 0)) for wd in widths],
        out_shape=[jax.ShapeDtypeStruct((n, wd), BF16) for wd in widths],
        compiler_params=_cparams("parallel"),
        name="inproj",
    )(x, g.reshape(1, d), *ws)


def _conv_kernel(c_ref, w_ref, o_ref):
    t = c_ref.shape[1]
    w = GROUP_WIDTH
    u = c_ref[0, :, 2 * w:3 * w].astype(F32) * c_ref[0, :, 0:w].astype(F32)
    row = lax.broadcasted_iota(jnp.int32, u.shape, 0)
    prev = jnp.where(row == 0, 0.0, pltpu.roll(u, 1, 0))
    nxt = jnp.where(row == t - 1, 0.0, pltpu.roll(u, t - 1, 0))
    y = prev * w_ref[0:1, :] + u * w_ref[1:2, :] + nxt * w_ref[2:3, :]
    o_ref[0] = c_ref[0, :, w:2 * w].astype(F32) * y


def _conv(c, conv_w):
    b, t, cw = c.shape
    return pl.pallas_call(
        _conv_kernel,
        grid=(b,),
        in_specs=[pl.BlockSpec((1, t, cw), lambda i: (i, 0, 0)), pl.BlockSpec((3, GROUP_WIDTH), lambda i: (0, 0))],
        out_specs=pl.BlockSpec((1, t, GROUP_WIDTH), lambda i: (i, 0, 0)),
        out_shape=jax.ShapeDtypeStruct((b, t, GROUP_WIDTH), F32),
        compiler_params=_cparams("parallel"),
        name="conv",
    )(c, conv_w)


def _softplus(z):
    return jnp.maximum(z, 0.0) + jnp.log(1.0 + jnp.exp(-jnp.abs(z)))


def _hdot(a, b):
    ah, al = _split_bf16(a)
    bh, bl = _split_bf16(b)
    dot = functools.partial(jnp.dot, preferred_element_type=F32)
    return dot(ah, bh) + (dot(ah, bl) + dot(al, bh))


SLAB = HEAD_DIM
PACK_R, PACK_KK, PACK_W, PACK_K, PACK_B = range(5)
PACK_COLS = 5


def _rwkv_prep_kernel(rw_ref, w0f, w0b, w2f, w2b, a0f, a0b, a2f, a2b, g2, kk_w, ka_w, rk_w,
                      pf_ref, pb_ref, g_ref, bonus_ref, vt_ref):
    w = GROUP_WIDTH

    def cols(lo, hi):
        return rw_ref[:, lo:hi].astype(F32)

    r, k, v = cols(0, w), cols(w, 2 * w), cols(2 * w, 3 * w)
    base = 3 * w
    wd = (cols(base, base + 64), cols(base + 64, base + 128))
    ad = (cols(base + 128, base + 192), cols(base + 192, base + 256))
    gd = cols(base + 256, base + 384)
    g_ref[...] = _hdot(jax.nn.sigmoid(gd), g2[...])
    kk = k * kk_w[...]
    kk = kk / jnp.maximum(jnp.sqrt(_head_sum(kk * kk)), 1e-12)
    k_sum = jnp.zeros_like(k)
    for d, (w0, w2, a0, a2, p_ref) in enumerate(((w0f, w2f, a0f, a2f, pf_ref), (w0b, w2b, a0b, a2b, pb_ref))):
        w_log = -_softplus(-(w0[...] + _hdot(jnp.tanh(wd[d]), w2[...]))) - 0.5
        decay = jnp.exp(-jnp.exp(w_log))
        a = jax.nn.sigmoid(a0[...] + _hdot(ad[d], a2[...]))
        k_dir = k * (1.0 + (a - 1.0) * ka_w[...])
        k_sum = k_sum + k_dir
        p_ref[:, PACK_R * w:(PACK_R + 1) * w] = r
        p_ref[:, PACK_KK * w:(PACK_KK + 1) * w] = kk
        p_ref[:, PACK_W * w:(PACK_W + 1) * w] = decay
        p_ref[:, PACK_K * w:(PACK_K + 1) * w] = k_dir
        p_ref[:, PACK_B * w:(PACK_B + 1) * w] = kk * a
    bonus_ref[...] = _head_sum(r * k_sum * rk_w[...]) * v
    for pair in range(vt_ref.shape[0] // 2):
        blocks = _swap_head_blocks(v[pair * 2 * SLAB:(pair + 1) * 2 * SLAB])
        vt_ref[2 * pair] = blocks[0]
        vt_ref[2 * pair + 1] = blocks[1]


def _rwkv_prep(rw, p, tm=512):
    n, cw = rw.shape
    w = GROUP_WIDTH
    row = lambda a: a.reshape(1, w)
    params = [row(p["w0_f"]), row(p["w0_b"]), p["w2_f"], p["w2_b"], row(p["a0_f"]), row(p["a0_b"]),
              p["a2_f"], p["a2_b"], p["g2"], row(p["k_k"]), row(p["k_a"]), row(p["r_k"])]
    full = lambda a: pl.BlockSpec(a.shape, lambda i: (0, 0))
    pack = pl.BlockSpec((tm, PACK_COLS * w), lambda i: (i, 0))
    tile = pl.BlockSpec((tm, w), lambda i: (i, 0))
    return pl.pallas_call(
        _rwkv_prep_kernel,
        grid=(n // tm,),
        in_specs=[pl.BlockSpec((tm, cw), lambda i: (i, 0))] + [full(a) for a in params],
        out_specs=[pack, pack, tile, tile, pl.BlockSpec((tm // SLAB, SLAB, w), lambda i: (i, 0, 0))],
        out_shape=[jax.ShapeDtypeStruct((n, PACK_COLS * w), F32), jax.ShapeDtypeStruct((n, PACK_COLS * w), F32),
                   jax.ShapeDtypeStruct((n, w), F32), jax.ShapeDtypeStruct((n, w), F32),
                   jax.ShapeDtypeStruct((n // SLAB, SLAB, w), F32)],
        compiler_params=_cparams("parallel"),
        name="rwkv_prep",
    )(rw, *params)


SCAN_GROUP = 4


def _rwkv_scan_kernel(pf_ref, pb_ref, vtf_ref, vtb_ref, yf_ref, yb_ref, s_ref, q_ref, *, nb):
    w = GROUP_WIDTH
    n = HEAD_DIM

    @pl.when(pl.program_id(0) == 0)
    def _():
        s_ref[...] = jnp.zeros_like(s_ref)
        q_ref[...] = jnp.zeros_like(q_ref)

    ones = _head_ones(w, n).astype(BF16)
    slot = lax.broadcasted_iota(jnp.int32, (n, w), 1) % n
    head_base = lax.broadcasted_iota(jnp.int32, (n, LANES), 1) // n * n

    def group_step(chains):
        def row(ch, col):
            return ch[1][ch[4], pl.ds(ch[5], 1), col * w:(col + 1) * w]

        pieces = []
        for ch in chains:
            pieces += [s_ref[ch[0]].astype(BF16) * row(ch, PACK_KK).astype(BF16), q_ref[ch[0]]]
        res = jnp.dot(jnp.concatenate(pieces, axis=0), ones, preferred_element_type=F32)
        for i, ch in enumerate(chains):
            c, _, vt_ref, y_ref, b, _, lanes, mask = ch
            u, y_prev = res[2 * i * n:(2 * i + 1) * n], res[(2 * i + 1) * n:(2 * i + 2) * n]
            vcol = jnp.concatenate([jnp.take_along_axis(vt_ref[b, 0, :, p * LANES:(p + 1) * LANES], lanes, axis=1)
                                    for p in range(w // LANES)], axis=1)
            s = s_ref[c] * row(ch, PACK_W) - u * row(ch, PACK_B) + vcol * row(ch, PACK_K)
            s_ref[c] = s
            q_ref[c] = s.astype(BF16) * row(ch, PACK_R).astype(BF16)
            pltpu.store(y_ref.at[b, 0], y_prev, mask=mask)

    def step(j, carry):
        mask_f = slot == j - 1
        mask_b = slot == SLAB - j
        lanes_f = head_base + j
        lanes_b = head_base + (SLAB - 1 - j)
        chains = [(b, pf_ref, vtf_ref, yf_ref, b, j, lanes_f, mask_f) for b in range(nb)]
        chains += [(nb + b, pb_ref, vtb_ref, yb_ref, b, SLAB - 1 - j, lanes_b, mask_b) for b in range(nb)]
        for g in range(0, 2 * nb, SCAN_GROUP):
            group_step(chains[g:g + SCAN_GROUP])
        return carry

    lax.fori_loop(0, SLAB, step, 0, unroll=4)
    for b in range(nb):
        for c, y_ref, last in ((b, yf_ref, SLAB - 1), (nb + b, yb_ref, 0)):
            y_last = jnp.dot(q_ref[c], ones, preferred_element_type=F32)
            pltpu.store(y_ref.at[b, 0], y_last, mask=slot == last)


def _rwkv_scan(pf, pb, vt):
    b, t, cw = pf.shape
    w = GROUP_WIDTH
    nt = t // SLAB
    out = jax.ShapeDtypeStruct((b, nt, HEAD_DIM, w), F32)
    slab_f = pl.BlockSpec((b, 1, HEAD_DIM, w), lambda j: (0, j, 0, 0))
    slab_b = pl.BlockSpec((b, 1, HEAD_DIM, w), lambda j: (0, nt - 1 - j, 0, 0))
    return pl.pallas_call(
        functools.partial(_rwkv_scan_kernel, nb=b),
        grid=(nt,),
        in_specs=[pl.BlockSpec((b, SLAB, cw), lambda j: (0, j, 0)),
                  pl.BlockSpec((b, SLAB, cw), lambda j: (0, nt - 1 - j, 0)), slab_f, slab_b],
        out_specs=[slab_f, slab_b],
        out_shape=[out, out],
        scratch_shapes=[pltpu.VMEM((2 * b, HEAD_DIM, w), F32), pltpu.VMEM((2 * b, HEAD_DIM, w), BF16)],
        compiler_params=_cparams("arbitrary"),
        name="rwkv_scan",
    )(pf, pb, vt, vt)


def _swap_head_blocks(z):
    n = HEAD_DIM
    zt = [z[:, 0:2 * n].T, z[:, 2 * n:4 * n].T]
    return [jnp.concatenate([zt[h // 2][(h % 2) * n:(h % 2 + 1) * n, i * n:(i + 1) * n]
                             for h in range(GROUP_HEADS)], axis=1) for i in range(2)]


def _rwkv_finish(yf_ref, yb_ref, g_ref, bonus_ref, lg_ref, lb_ref, o_ref):
    n = HEAD_DIM
    for pair in range(yf_ref.shape[1] // 2):
        zs = []
        for s in (2 * pair, 2 * pair + 1):
            y = yf_ref[0, s] + yb_ref[0, s]
            d = y - jnp.mean(y, axis=0, keepdims=True)
            var = jnp.mean(d * d, axis=0, keepdims=True)
            zs.append(d * lax.rsqrt(var + RWKV_GN_EPS))
        for s, yn in zip((2 * pair, 2 * pair + 1), _swap_head_blocks(jnp.concatenate(zs, axis=0))):
            rows = slice(s * n, (s + 1) * n)
            o_ref[rows, :] = (yn * lg_ref[...] + lb_ref[...] + bonus_ref[rows, :]) * g_ref[rows, :]


def _ret_kernel(r_ref, pos_ref, gn_ref, o_ref, q_s, kt_s, v_s, o_s):
    w = GROUP_WIDTH
    n = HEAD_DIM
    c = RET_CHUNK
    nc = r_ref.shape[1] // c
    cos, sin = _rope_tables(pos_ref[0], n // 2)
    q_s[...] = _apply_rope(r_ref[0, :, 0:w].astype(F32), cos, sin, n // 2) * (n ** -0.5)
    kt_s[...] = _apply_rope(r_ref[0, :, w:2 * w].astype(F32), cos, sin, n // 2).T
    v_s[...] = r_ref[0, :, 2 * w:3 * w].astype(BF16)
    dist = jnp.abs(lax.broadcasted_iota(jnp.int32, (c, c), 0) - lax.broadcasted_iota(jnp.int32, (c, c), 1)).astype(F32)
    i_col = lax.broadcasted_iota(jnp.int32, (c, n), 0).astype(F32)
    j_row = lax.broadcasted_iota(jnp.int32, (n, c), 1).astype(F32)
    for h in range(GROUP_HEADS):
        lg = RET_LOG_GAMMA[h]
        hs = slice(h * n, (h + 1) * n)
        chunk_decay = math.exp(lg * c)
        gam = jnp.exp(lg * dist)
        state = jnp.zeros((n, n), F32)
        for ch in range(nc):
            ts = slice(ch * c, (ch + 1) * c)
            qc, kc, vc = q_s[ts, hs], kt_s[hs, ts], v_s[ts, hs]
            s = jnp.dot(qc.astype(BF16), kc.astype(BF16), preferred_element_type=F32) * gam
            o = jnp.dot(s.astype(BF16), vc, preferred_element_type=F32)
            o_s[ts, hs] = o + jnp.dot((qc * jnp.exp(lg * i_col)).astype(BF16), state.astype(BF16),
                                      preferred_element_type=F32)
            state = state * chunk_decay + jnp.dot((kc * jnp.exp(lg * (c - j_row))).astype(BF16), vc,
                                                  preferred_element_type=F32)
        state = jnp.zeros((n, n), F32)
        for ch in reversed(range(nc)):
            ts = slice(ch * c, (ch + 1) * c)
            qc, kc, vc = q_s[ts, hs], kt_s[hs, ts], v_s[ts, hs]
            o_s[ts, hs] += jnp.dot((qc * jnp.exp(lg * (c - i_col))).astype(BF16), state.astype(BF16),
                                   preferred_element_type=F32)
            state = state * chunk_decay + jnp.dot((kc * jnp.exp(lg * j_row)).astype(BF16), vc,
                                                  preferred_element_type=F32)
    o = o_s[...]
    o = o * lax.rsqrt(_head_sum(o * o) * (1.0 / n) + NORM_EPS) * gn_ref[...]
    g = r_ref[0, :, 3 * w:4 * w].astype(F32)
    o_ref[0] = g * jax.nn.sigmoid(g) * o


def _ret(rt, pos, gn_g):
    b, t, cw = rt.shape
    w = GROUP_WIDTH
    return pl.pallas_call(
        _ret_kernel,
        grid=(b,),
        in_specs=[pl.BlockSpec((1, t, cw), lambda i: (i, 0, 0)),
                  pl.BlockSpec((1, t, 1), lambda i: (i, 0, 0)),
                  pl.BlockSpec((1, w), lambda i: (0, 0))],
        out_specs=pl.BlockSpec((1, t, w), lambda i: (i, 0, 0)),
        out_shape=jax.ShapeDtypeStruct((b, t, w), F32),
        scratch_shapes=[pltpu.VMEM((t, w), F32), pltpu.VMEM((w, t), F32), pltpu.VMEM((t, w), BF16),
                        pltpu.VMEM((t, w), F32)],
        compiler_params=_cparams("parallel"),
        name="retention",
    )(rt, pos, gn_g.reshape(1, w))


def _mla_kernel(mq_ref, mkv_ref, pos_ref, qn_ref, qb_ref, kvn_ref, kvb_ref, o_ref, kp_s, v_s, cos_s, sin_s, *, tq):
    w = GROUP_WIDTH
    kw = GROUP_HEADS * LANES
    qi = pl.program_id(1)
    kv0 = MLA_Q_RANK
    half = MLA_ROPE // 2

    @pl.when(qi == 0)
    def _():
        cos, sin = _rope_tables(pos_ref[0], half)
        ckv = _rms(mkv_ref[0, :, kv0:kv0 + MLA_KV_RANK].astype(F32), kvn_ref[...]).astype(BF16)
        kv = jnp.dot(ckv, kvb_ref[...], preferred_element_type=F32)
        v_s[...] = kv[:, kw:kw + w].astype(BF16)
        kr = _apply_rope(mkv_ref[0, :, kv0 + MLA_KV_RANK:MLA_PAD].astype(F32), cos, sin, half)
        kr = pltpu.roll(kr, MLA_NOPE, 1)
        kp_s[...] = (kv[:, 0:kw] + _tile_lanes(kr, kw)).astype(BF16)
        lane = lax.broadcasted_iota(jnp.int32, (1, LANES), 1)
        rope_lane = (lane >= MLA_NOPE) & (lane < MLA_NOPE + MLA_ROPE)
        cos_s[...] = jnp.where(rope_lane, cos, 1.0)
        sin_s[...] = jnp.where(rope_lane, sin, 0.0)

    rows = pl.ds(pl.multiple_of(qi * tq, tq), tq)
    qa = _rms(mq_ref[0, :, 0:MLA_Q_RANK].astype(F32), qn_ref[...]).astype(BF16)
    q = jnp.dot(qa, qb_ref[...], preferred_element_type=F32)
    scale = (MLA_NOPE + MLA_ROPE) ** -0.5 * math.log2(math.e)
    q = (_apply_rope(q, cos_s[rows, :], sin_s[rows, :], half) * scale).astype(BF16)
    nt = (((1,), (1,)), ((), ()))
    outs = []
    for h in range(GROUP_HEADS):
        hl = slice(h * LANES, (h + 1) * LANES)
        s = lax.dot_general(q[:, hl], kp_s[:, hl], nt, preferred_element_type=F32)
        p = jnp.exp2(s - jnp.max(s, axis=-1, keepdims=True))
        den = jnp.sum(p, axis=-1, keepdims=True)
        outs.append(jnp.dot(p.astype(BF16), v_s[:, h * MLA_NOPE:(h + 1) * MLA_NOPE], preferred_element_type=F32) / den)
    o_ref[0] = jnp.concatenate(outs, axis=-1)


def _mla(ml, pos, qn_g, qb, kvn_g, kvb, tq=512):
    b, t, cw = ml.shape
    w = GROUP_WIDTH
    tq = min(tq, t)
    full = lambda a: pl.BlockSpec(a.shape, lambda i, j: (0, 0))
    qn_g = qn_g.reshape(1, -1)
    kvn_g = kvn_g.reshape(1, -1)
    return pl.pallas_call(
        functools.partial(_mla_kernel, tq=tq),
        grid=(b, t // tq),
        in_specs=[pl.BlockSpec((1, tq, cw), lambda i, j: (i, j, 0)),
                  pl.BlockSpec((1, t, cw), lambda i, j: (i, 0, 0)),
                  pl.BlockSpec((1, t, 1), lambda i, j: (i, 0, 0)),
                  full(qn_g), full(qb), full(kvn_g), full(kvb)],
        out_specs=pl.BlockSpec((1, tq, w), lambda i, j: (i, j, 0)),
        out_shape=jax.ShapeDtypeStruct((b, t, w), F32),
        scratch_shapes=[pltpu.VMEM((t, GROUP_HEADS * LANES), BF16), pltpu.VMEM((t, w), BF16),
                        pltpu.VMEM((t, LANES), F32), pltpu.VMEM((t, LANES), F32)],
        compiler_params=_cparams("parallel", "arbitrary"),
        name="mla",
    )(ml, ml, pos, qn_g, qb, kvn_g, kvb)


def _mla_weights(q_b, kv_b):
    qh = q_b.reshape(MLA_Q_RANK, GROUP_HEADS, MLA_NOPE + MLA_ROPE)
    qb = jnp.pad(qh, ((0, 0), (0, 0), (0, LANES - MLA_NOPE - MLA_ROPE))).reshape(MLA_Q_RANK, GROUP_HEADS * LANES)
    kvh = kv_b.reshape(MLA_KV_RANK, GROUP_HEADS, 2 * MLA_NOPE)
    kn = jnp.pad(kvh[:, :, :MLA_NOPE], ((0, 0), (0, 0), (0, LANES - MLA_NOPE))).reshape(MLA_KV_RANK, GROUP_HEADS * LANES)
    v = kvh[:, :, MLA_NOPE:].reshape(MLA_KV_RANK, GROUP_HEADS * MLA_NOPE)
    return qb.astype(BF16), jnp.concatenate([kn, v], axis=1).astype(BF16)


def _outproj_kernel(x_ref, yc_ref, yf_ref, yb_ref, g_ref, bonus_ref, lg_ref, lb_ref, yr_ref, ym_ref, w_ref, o_ref,
                    yw_s):
    w = GROUP_WIDTH
    _rwkv_finish(yf_ref, yb_ref, g_ref, bonus_ref, lg_ref, lb_ref, yw_s)
    acc = x_ref[...]
    for g, y in enumerate((yc_ref, yw_s, yr_ref, ym_ref)):
        acc = acc + jnp.dot(y[...].astype(BF16), w_ref[g * w:(g + 1) * w, :], preferred_element_type=F32)
    o_ref[...] = acc


def _outproj(x, y_conv, rwkv, y_ret, y_mla, w_out, tm=1024):
    yf, yb, g, bonus, lnx_g, lnx_b = rwkv
    n, d = x.shape
    w = GROUP_WIDTH
    b, nt, hd, _ = yf.shape
    tm = min(tm, nt * hd)
    per_seq = nt * hd // tm
    ytile = pl.BlockSpec((tm, w), lambda i: (i, 0))
    slabs = pl.BlockSpec((1, tm // hd, hd, w), lambda i: (i // per_seq, i % per_seq, 0, 0))
    vec = pl.BlockSpec((1, w), lambda i: (0, 0))
    return pl.pallas_call(
        _outproj_kernel,
        grid=(n // tm,),
        in_specs=[pl.BlockSpec((tm, d), lambda i: (i, 0)), ytile, slabs, slabs, ytile, ytile, vec, vec, ytile, ytile,
                  pl.BlockSpec(w_out.shape, lambda i: (0, 0))],
        out_specs=pl.BlockSpec((tm, d), lambda i: (i, 0)),
        out_shape=jax.ShapeDtypeStruct((n, d), F32),
        scratch_shapes=[pltpu.VMEM((tm, w), F32)],
        compiler_params=_cparams("parallel"),
        name="outproj",
    )(x, y_conv, yf, yb, g, bonus, lnx_g.reshape(1, w), lnx_b.reshape(1, w), y_ret, y_mla, w_out)


def kernel(x, positions, ffn1_norm, ffn1_w_gate, ffn1_w_up, ffn1_w_down, mix_norm, w_in, w_out, conv_w, rwkv_w0_f, rwkv_w0_b, rwkv_w2_f, rwkv_w2_b, rwkv_a0_f, rwkv_a0_b, rwkv_a2_f, rwkv_a2_b, rwkv_g2, rwkv_k_k, rwkv_k_a, rwkv_r_k, rwkv_lnx_g, rwkv_lnx_b, ret_gn_g, mla_q_a_norm, mla_q_b, mla_kv_a_norm, mla_kv_b, ffn2_norm, ffn2_w_gate, ffn2_w_up, ffn2_w_down, final_norm):
    b, t, d = x.shape
    n = b * t
    w = GROUP_WIDTH
    depth = w_in.shape[0]
    pos = positions.astype(F32).reshape(b, t, 1)
    xf = x.reshape(n, d)
    for l in range(depth):
        xf = _ffn(xf, ffn1_norm[l], ffn1_w_gate[l].astype(BF16), ffn1_w_up[l].astype(BF16),
                  ffn1_w_down[l].astype(BF16))
        wl = w_in[l].astype(BF16)
        w_mla = jnp.pad(wl[:, MLA_COLS[0]:MLA_COLS[1]], ((0, 0), (0, MLA_PAD - (MLA_COLS[1] - MLA_COLS[0]))))
        ws = [wl[:, CONV_COLS[0]:CONV_COLS[1]], wl[:, RWKV_COLS[0]:RWKV_COLS[1]], wl[:, RET_COLS[0]:RET_COLS[1]], w_mla]
        cv, rw, rt, ml = _inproj(xf, mix_norm[l], ws)
        y_conv = _conv(cv.reshape(b, t, -1), conv_w[l])
        rp = dict(w0_f=rwkv_w0_f[l], w0_b=rwkv_w0_b[l], w2_f=rwkv_w2_f[l], w2_b=rwkv_w2_b[l],
                  a0_f=rwkv_a0_f[l], a0_b=rwkv_a0_b[l], a2_f=rwkv_a2_f[l], a2_b=rwkv_a2_b[l],
                  g2=rwkv_g2[l], k_k=rwkv_k_k[l], k_a=rwkv_k_a[l], r_k=rwkv_r_k[l])
        pf, pb, g, bonus, vt = _rwkv_prep(rw, rp)
        yf, yb = _rwkv_scan(pf.reshape(b, t, -1), pb.reshape(b, t, -1), vt.reshape(b, t // SLAB, SLAB, w))
        y_ret = _ret(rt.reshape(b, t, -1), pos, ret_gn_g[l])
        qb, kvb = _mla_weights(mla_q_b[l], mla_kv_b[l])
        y_mla = _mla(ml.reshape(b, t, -1), pos, mla_q_a_norm[l], qb, mla_kv_a_norm[l], kvb)
        xf = _outproj(xf, y_conv.reshape(n, w), (yf, yb, g, bonus, rwkv_lnx_g[l], rwkv_lnx_b[l]),
                      y_ret.reshape(n, w), y_mla.reshape(n, w), w_out[l].astype(BF16))
        xf = _ffn(xf, ffn2_norm[l], ffn2_w_gate[l].astype(BF16), ffn2_w_up[l].astype(BF16),
                  ffn2_w_down[l].astype(BF16), final_g=final_norm if l == depth - 1 else None)
    return xf.reshape(b, t, d)
```

```python
import functools
import math

import jax
import jax.numpy as jnp
from jax import lax
from jax.experimental import pallas as pl
from jax.experimental.pallas import tpu as pltpu

F32 = jnp.float32
BF16 = jnp.bfloat16

D_MODEL = 1024
GROUP_HEADS = 4
GROUP_WIDTH = 256
HEAD_DIM = 64
D_FF = 2816
RWKV_GN_EPS = 64e-5
MLA_Q_RANK = 384
MLA_KV_RANK = 128
MLA_NOPE = 64
MLA_ROPE = 32
ROPE_BASE = 10000.0
NORM_EPS = 1e-6
RET_LOG_GAMMA = tuple(math.log(1.0 - 2.0 ** (-5.0 - h)) for h in range(GROUP_HEADS))
RET_CHUNK = 128

CONV_COLS = (0, 768)
RWKV_COLS = (768, 1920)
RET_COLS = (1920, 2944)
MLA_COLS = (2944, 3488)
MLA_PAD = 640

VMEM_LIMIT = 56 * 1024 * 1024
LANES = 128
MXU_TILE = 256


def _cparams(*sem):
    return pltpu.CompilerParams(dimension_semantics=sem, vmem_limit_bytes=VMEM_LIMIT)


def _rms(x, g):
    return x * lax.rsqrt(jnp.mean(x * x, axis=-1, keepdims=True) + NORM_EPS) * g


def _head_ones(width, head):
    r = lax.broadcasted_iota(jnp.int32, (width, width), 0) // head
    c = lax.broadcasted_iota(jnp.int32, (width, width), 1) // head
    return r == c


def _split_bf16(x):
    hi = x.astype(BF16)
    return hi, (x - hi.astype(F32)).astype(BF16)


def _head_sum(x, head=HEAD_DIM):
    g = _head_ones(x.shape[-1], head).astype(BF16)
    hi, lo = _split_bf16(x)
    return jnp.dot(hi, g, preferred_element_type=F32) + jnp.dot(lo, g, preferred_element_type=F32)


def _rotate_half(x, half):
    n = x.shape[-1]
    lane = lax.broadcasted_iota(jnp.int32, x.shape, x.ndim - 1)
    first = (lane % (2 * half)) < half
    from_right = pltpu.roll(x, n - half, x.ndim - 1)
    from_left = pltpu.roll(x, half, x.ndim - 1)
    return jnp.where(first, -from_right, from_left)


def _rope_tables(pos, half):
    lane = lax.broadcasted_iota(jnp.int32, (1, LANES), 1)
    inv = jnp.exp((lane % half).astype(F32) * (-math.log(ROPE_BASE) / half))
    ang = pos * inv
    return jnp.cos(ang), jnp.sin(ang)


def _tile_lanes(x, width):
    return x if width == x.shape[-1] else jnp.concatenate([x] * (width // x.shape[-1]), axis=-1)


def _apply_rope(x, cos, sin, half):
    width = x.shape[-1]
    return x * _tile_lanes(cos, width) + _rotate_half(x, half) * _tile_lanes(sin, width)


def _ffn_kernel(x_ref, g_ref, wg_ref, wu_ref, wd_ref, *rest, chunks, final):
    if final:
        fg_ref, o_ref = rest
    else:
        (o_ref,) = rest
    x = x_ref[...]
    h = _rms(x, g_ref[...]).astype(BF16)
    acc = None
    f0 = 0
    for fc in chunks:
        cols = slice(f0, f0 + fc)
        gate = jnp.dot(h, wg_ref[:, cols], preferred_element_type=F32)
        up = jnp.dot(h, wu_ref[:, cols], preferred_element_type=F32)
        act = (gate * jax.nn.sigmoid(gate) * up).astype(BF16)
        part = jnp.dot(act, wd_ref[cols, :], preferred_element_type=F32)
        acc = part if acc is None else acc + part
        f0 += fc
    y = x + 0.5 * acc
    if final:
        y = _rms(y, fg_ref[...])
    o_ref[...] = y


def _ffn(x, g, wg, wu, wd, final_g=None, tm=512, max_chunk=6 * MXU_TILE):
    n, d = x.shape
    f = wg.shape[1]
    chunks = [max_chunk] * (f // max_chunk) + ([f % max_chunk] if f % max_chunk else [])
    final = final_g is not None
    resident = lambda shape: pl.BlockSpec(shape, lambda i: (0, 0), pipeline_mode=pl.Buffered(1))
    in_specs = [
        pl.BlockSpec((tm, d), lambda i: (i, 0)),
        pl.BlockSpec((1, d), lambda i: (0, 0)),
        resident((d, f)), resident((d, f)), resident((f, d)),
    ]
    args = [x, g.reshape(1, d), wg, wu, wd]
    if final:
        in_specs.append(pl.BlockSpec((1, d), lambda i: (0, 0)))
        args.append(final_g.reshape(1, d))
    return pl.pallas_call(
        functools.partial(_ffn_kernel, chunks=tuple(chunks), final=final),
        grid=(n // tm,),
        in_specs=in_specs,
        out_specs=pl.BlockSpec((tm, d), lambda i: (i, 0)),
        out_shape=jax.ShapeDtypeStruct((n, d), F32),
        compiler_params=_cparams("parallel"),
        name="ffn",
    )(*args)


def _inproj_kernel(x_ref, g_ref, w1, w2, w3, w4, o1, o2, o3, o4):
    h = _rms(x_ref[...], g_ref[...]).astype(BF16)
    for w, o in ((w1, o1), (w2, o2), (w3, o3), (w4, o4)):
        o[...] = jnp.dot(h, w[...], preferred_element_type=F32).astype(o.dtype)


def _inproj(x, g, ws, tm=512):
    n, d = x.shape
    widths = [w.shape[1] for w in ws]
    return pl.pallas_call(
        _inproj_kernel,
        grid=(n // tm,),
        in_specs=[pl.BlockSpec((tm, d), lambda i: (i, 0)), pl.BlockSpec((1, d), lambda i: (0, 0))]
        + [pl.BlockSpec((d, wd), lambda i: (0, 0)) for wd in widths],
        out_specs=[pl.BlockSpec((tm, wd), lambda i: (i, 0)) for wd in widths],
        out_shape=[jax.ShapeDtypeStruct((n, wd), BF16) for wd in widths],
        compiler_params=_cparams("parallel"),
        name="inproj",
    )(x, g.reshape(1, d), *ws)


def _conv_kernel(c_ref, w_ref, o_ref):
    t = c_ref.shape[1]
    w = GROUP_WIDTH
    u = c_ref[0, :, 2 * w:3 * w].astype(F32) * c_ref[0, :, 0:w].astype(F32)
    row = lax.broadcasted_iota(jnp.int32, u.shape, 0)
    prev = jnp.where(row == 0, 0.0, pltpu.roll(u, 1, 0))
    nxt = jnp.where(row == t - 1, 0.0, pltpu.roll(u, t - 1, 0))
    y = prev * w_ref[0:1, :] + u * w_ref[1:2, :] + nxt * w_ref[2:3, :]
    o_ref[0] = c_ref[0, :, w:2 * w].astype(F32) * y


def _conv(c, conv_w):
    b, t, cw = c.shape
    return pl.pallas_call(
        _conv_kernel,
        grid=(b,),
        in_specs=[pl.BlockSpec((1, t, cw), lambda i: (i, 0, 0)), pl.BlockSpec((3, GROUP_WIDTH), lambda i: (0, 0))],
        out_specs=pl.BlockSpec((1, t, GROUP_WIDTH), lambda i: (i, 0, 0)),
        out_shape=jax.ShapeDtypeStruct((b, t, GROUP_WIDTH), F32),
        compiler_params=_cparams("parallel"),
        name="conv",
    )(c, conv_w)


def _softplus(z):
    return jnp.maximum(z, 0.0) + jnp.log(1.0 + jnp.exp(-jnp.abs(z)))


def _hdot(a, b):
    ah, al = _split_bf16(a)
    bh, bl = _split_bf16(b)
    dot = functools.partial(jnp.dot, preferred_element_type=F32)
    return dot(ah, bh) + (dot(ah, bl) + dot(al, bh))


SLAB = HEAD_DIM
PACK_R, PACK_KK, PACK_W, PACK_K, PACK_B, PACK_V = range(6)
PACK_COLS = 6


def _rwkv_prep_kernel(rw_ref, w0f, w0b, w2f, w2b, a0f, a0b, a2f, a2b, g2, kk_w, ka_w, rk_w,
                      pf_ref, pb_ref, g_ref, bonus_ref, vt_ref):
    w = GROUP_WIDTH

    def cols(lo, hi):
        return rw_ref[:, lo:hi].astype(F32)

    r, k, v = cols(0, w), cols(w, 2 * w), cols(2 * w, 3 * w)
    base = 3 * w
    wd = (cols(base, base + 64), cols(base + 64, base + 128))
    ad = (cols(base + 128, base + 192), cols(base + 192, base + 256))
    gd = cols(base + 256, base + 384)
    g_ref[...] = _hdot(jax.nn.sigmoid(gd), g2[...])
    kk = k * kk_w[...]
    kk = kk / jnp.maximum(jnp.sqrt(_head_sum(kk * kk)), 1e-12)
    k_sum = jnp.zeros_like(k)
    for d, (w0, w2, a0, a2, p_ref) in enumerate(((w0f, w2f, a0f, a2f, pf_ref), (w0b, w2b, a0b, a2b, pb_ref))):
        w_log = -_softplus(-(w0[...] + _hdot(jnp.tanh(wd[d]), w2[...]))) - 0.5
        decay = jnp.exp(-jnp.exp(w_log))
        a = jax.nn.sigmoid(a0[...] + _hdot(ad[d], a2[...]))
        k_dir = k * (1.0 + (a - 1.0) * ka_w[...])
        k_sum = k_sum + k_dir
        p_ref[:, PACK_R * w:(PACK_R + 1) * w] = r
        p_ref[:, PACK_KK * w:(PACK_KK + 1) * w] = kk
        p_ref[:, PACK_W * w:(PACK_W + 1) * w] = decay
        p_ref[:, PACK_K * w:(PACK_K + 1) * w] = k_dir
        p_ref[:, PACK_B * w:(PACK_B + 1) * w] = kk * a
        p_ref[:, PACK_V * w:(PACK_V + 1) * w] = v
    bonus_ref[...] = _head_sum(r * k_sum * rk_w[...]) * v
    for pair in range(vt_ref.shape[0] // 2):
        blocks = _swap_head_blocks(v[pair * 2 * SLAB:(pair + 1) * 2 * SLAB])
        vt_ref[2 * pair] = blocks[0]
        vt_ref[2 * pair + 1] = blocks[1]


def _rwkv_prep(rw, p, tm=512):
    n, cw = rw.shape
    w = GROUP_WIDTH
    row = lambda a: a.reshape(1, w)
    params = [row(p["w0_f"]), row(p["w0_b"]), p["w2_f"], p["w2_b"], row(p["a0_f"]), row(p["a0_b"]),
              p["a2_f"], p["a2_b"], p["g2"], row(p["k_k"]), row(p["k_a"]), row(p["r_k"])]
    full = lambda a: pl.BlockSpec(a.shape, lambda i: (0, 0))
    pack = pl.BlockSpec((tm, PACK_COLS * w), lambda i: (i, 0))
    tile = pl.BlockSpec((tm, w), lambda i: (i, 0))
    return pl.pallas_call(
        _rwkv_prep_kernel,
        grid=(n // tm,),
        in_specs=[pl.BlockSpec((tm, cw), lambda i: (i, 0))] + [full(a) for a in params],
        out_specs=[pack, pack, tile, tile, pl.BlockSpec((tm // SLAB, SLAB, w), lambda i: (i, 0, 0))],
        out_shape=[jax.ShapeDtypeStruct((n, PACK_COLS * w), F32), jax.ShapeDtypeStruct((n, PACK_COLS * w), F32),
                   jax.ShapeDtypeStruct((n, w), F32), jax.ShapeDtypeStruct((n, w), F32),
                   jax.ShapeDtypeStruct((n // SLAB, SLAB, w), F32)],
        compiler_params=_cparams("parallel"),
        name="rwkv_prep",
    )(rw, *params)


SCAN_GROUP = 4


def _rwkv_scan_kernel(pf_ref, pb_ref, vtb_ref, yf_ref, yb_ref, s_ref, q_ref, *, nb):
    w = GROUP_WIDTH
    n = HEAD_DIM

    @pl.when(pl.program_id(0) == 0)
    def _():
        s_ref[...] = jnp.zeros_like(s_ref)
        q_ref[...] = jnp.zeros_like(q_ref)

    ones = _head_ones(w, n).astype(BF16)
    slot = lax.broadcasted_iota(jnp.int32, (n, w), 1) % n
    diag = (lax.broadcasted_iota(jnp.int32, (n, w), 0) == slot).astype(BF16)
    head_base = lax.broadcasted_iota(jnp.int32, (n, LANES), 1) // n * n

    def group_step(chains):
        def row(ch, col):
            return ch[1][ch[4], pl.ds(ch[5], 1), col * w:(col + 1) * w]

        pieces, at = [], []
        for ch in chains:
            at.append(len(pieces))
            pieces += [s_ref[ch[0]].astype(BF16) * row(ch, PACK_KK).astype(BF16), q_ref[ch[0]]]
            if ch[2] is None:
                pieces.append(diag * row(ch, PACK_V).astype(BF16))
        res = jnp.dot(jnp.concatenate(pieces, axis=0), ones, preferred_element_type=F32)
        for i, ch in enumerate(chains):
            c, _, vt_ref, y_ref, b, _, lanes, mask = ch
            u, y_prev = res[at[i] * n:(at[i] + 1) * n], res[(at[i] + 1) * n:(at[i] + 2) * n]
            if vt_ref is None:
                vcol = res[(at[i] + 2) * n:(at[i] + 3) * n]
            else:
                vcol = jnp.concatenate([jnp.take_along_axis(vt_ref[b, 0, :, p * LANES:(p + 1) * LANES], lanes, axis=1)
                                        for p in range(w // LANES)], axis=1)
            s = s_ref[c] * row(ch, PACK_W) - u * row(ch, PACK_B) + vcol * row(ch, PACK_K)
            s_ref[c] = s
            q_ref[c] = s.astype(BF16) * row(ch, PACK_R).astype(BF16)
            pltpu.store(y_ref.at[b, 0], y_prev, mask=mask)

    def step(j, carry):
        mask_f = slot == j - 1
        mask_b = slot == SLAB - j
        lanes_b = head_base + (SLAB - 1 - j)
        chains = [(b, pf_ref, None, yf_ref, b, j, None, mask_f) for b in range(nb)]
        chains += [(nb + b, pb_ref, vtb_ref, yb_ref, b, SLAB - 1 - j, lanes_b, mask_b) for b in range(nb)]
        for g in range(0, 2 * nb, SCAN_GROUP):
            group_step(chains[g:g + SCAN_GROUP])
        return carry

    lax.fori_loop(0, SLAB, step, 0, unroll=4)
    for b in range(nb):
        for c, y_ref, last in ((b, yf_ref, SLAB - 1), (nb + b, yb_ref, 0)):
            y_last = jnp.dot(q_ref[c], ones, preferred_element_type=F32)
            pltpu.store(y_ref.at[b, 0], y_last, mask=slot == last)


def _rwkv_scan(pf, pb, vt):
    b, t, cw = pf.shape
    w = GROUP_WIDTH
    nt = t // SLAB
    out = jax.ShapeDtypeStruct((b, nt, HEAD_DIM, w), F32)
    slab_f = pl.BlockSpec((b, 1, HEAD_DIM, w), lambda j: (0, j, 0, 0))
    slab_b = pl.BlockSpec((b, 1, HEAD_DIM, w), lambda j: (0, nt - 1 - j, 0, 0))
    return pl.pallas_call(
        functools.partial(_rwkv_scan_kernel, nb=b),
        grid=(nt,),
        in_specs=[pl.BlockSpec((b, SLAB, cw), lambda j: (0, j, 0)),
                  pl.BlockSpec((b, SLAB, cw), lambda j: (0, nt - 1 - j, 0)), slab_b],
        out_specs=[slab_f, slab_b],
        out_shape=[out, out],
        scratch_shapes=[pltpu.VMEM((2 * b, HEAD_DIM, w), F32), pltpu.VMEM((2 * b, HEAD_DIM, w), BF16)],
        compiler_params=_cparams("arbitrary"),
        name="rwkv_scan",
    )(pf, pb, vt)


def _swap_head_blocks(z):
    n = HEAD_DIM
    zt = [z[:, 0:2 * n].T, z[:, 2 * n:4 * n].T]
    return [jnp.concatenate([zt[h // 2][(h % 2) * n:(h % 2 + 1) * n, i * n:(i + 1) * n]
                             for h in range(GROUP_HEADS)], axis=1) for i in range(2)]


def _rwkv_finish(yf_ref, yb_ref, g_ref, bonus_ref, lg_ref, lb_ref, o_ref):
    n = HEAD_DIM
    for pair in range(yf_ref.shape[1] // 2):
        zs = []
        for s in (2 * pair, 2 * pair + 1):
            y = yf_ref[0, s] + yb_ref[0, s]
            d = y - jnp.mean(y, axis=0, keepdims=True)
            var = jnp.mean(d * d, axis=0, keepdims=True)
            zs.append(d * lax.rsqrt(var + RWKV_GN_EPS))
        for s, yn in zip((2 * pair, 2 * pair + 1), _swap_head_blocks(jnp.concatenate(zs, axis=0))):
            rows = slice(s * n, (s + 1) * n)
            o_ref[rows, :] = (yn * lg_ref[...] + lb_ref[...] + bonus_ref[rows, :]) * g_ref[rows, :]


def _ret_kernel(r_ref, pos_ref, gn_ref, o_ref, q_s, kt_s, v_s, o_s):
    w = GROUP_WIDTH
    n = HEAD_DIM
    c = RET_CHUNK
    nc = r_ref.shape[1] // c
    cos, sin = _rope_tables(pos_ref[0], n // 2)
    q_s[...] = _apply_rope(r_ref[0, :, 0:w].astype(F32), cos, sin, n // 2) * (n ** -0.5)
    kt_s[...] = _apply_rope(r_ref[0, :, w:2 * w].astype(F32), cos, sin, n // 2).T
    v_s[...] = r_ref[0, :, 2 * w:3 * w].astype(BF16)
    dist = jnp.abs(lax.broadcasted_iota(jnp.int32, (c, c), 0) - lax.broadcasted_iota(jnp.int32, (c, c), 1)).astype(F32)
    i_col = lax.broadcasted_iota(jnp.int32, (c, n), 0).astype(F32)
    j_row = lax.broadcasted_iota(jnp.int32, (n, c), 1).astype(F32)
    for h in range(GROUP_HEADS):
        lg = RET_LOG_GAMMA[h]
        hs = slice(h * n, (h + 1) * n)
        chunk_decay = math.exp(lg * c)
        gam = jnp.exp(lg * dist)
        state = jnp.zeros((n, n), F32)
        for ch in range(nc):
            ts = slice(ch * c, (ch + 1) * c)
            qc, kc, vc = q_s[ts, hs], kt_s[hs, ts], v_s[ts, hs]
            s = jnp.dot(qc.astype(BF16), kc.astype(BF16), preferred_element_type=F32) * gam
            o = jnp.dot(s.astype(BF16), vc, preferred_element_type=F32)
            o_s[ts, hs] = o + jnp.dot((qc * jnp.exp(lg * i_col)).astype(BF16), state.astype(BF16),
                                      preferred_element_type=F32)
            state = state * chunk_decay + jnp.dot((kc * jnp.exp(lg * (c - j_row))).astype(BF16), vc,
                                                  preferred_element_type=F32)
        state = jnp.zeros((n, n), F32)
        for ch in reversed(range(nc)):
            ts = slice(ch * c, (ch + 1) * c)
            qc, kc, vc = q_s[ts, hs], kt_s[hs, ts], v_s[ts, hs]
            o_s[ts, hs] += jnp.dot((qc * jnp.exp(lg * (c - i_col))).astype(BF16), state.astype(BF16),
                                   preferred_element_type=F32)
            state = state * chunk_decay + jnp.dot((kc * jnp.exp(lg * j_row)).astype(BF16), vc,
                                                  preferred_element_type=F32)
    o = o_s[...]
    o = o * lax.rsqrt(_head_sum(o * o) * (1.0 / n) + NORM_EPS) * gn_ref[...]
    g = r_ref[0, :, 3 * w:4 * w].astype(F32)
    o_ref[0] = g * jax.nn.sigmoid(g) * o


def _ret(rt, pos, gn_g):
    b, t, cw = rt.shape
    w = GROUP_WIDTH
    return pl.pallas_call(
        _ret_kernel,
        grid=(b,),
        in_specs=[pl.BlockSpec((1, t, cw), lambda i: (i, 0, 0)),
                  pl.BlockSpec((1, t, 1), lambda i: (i, 0, 0)),
                  pl.BlockSpec((1, w), lambda i: (0, 0))],
        out_specs=pl.BlockSpec((1, t, w), lambda i: (i, 0, 0)),
        out_shape=jax.ShapeDtypeStruct((b, t, w), F32),
        scratch_shapes=[pltpu.VMEM((t, w), F32), pltpu.VMEM((w, t), F32), pltpu.VMEM((t, w), BF16),
                        pltpu.VMEM((t, w), F32)],
        compiler_params=_cparams("parallel"),
        name="retention",
    )(rt, pos, gn_g.reshape(1, w))


def _mla_kernel(mq_ref, mkv_ref, pos_ref, qn_ref, qb_ref, kvn_ref, kvb_ref, o_ref, kp_s, v_s, cos_s, sin_s, *, tq):
    w = GROUP_WIDTH
    kw = GROUP_HEADS * LANES
    qi = pl.program_id(1)
    kv0 = MLA_Q_RANK
    half = MLA_ROPE // 2

    @pl.when(qi == 0)
    def _():
        cos, sin = _rope_tables(pos_ref[0], half)
        ckv = _rms(mkv_ref[0, :, kv0:kv0 + MLA_KV_RANK].astype(F32), kvn_ref[...]).astype(BF16)
        kv = jnp.dot(ckv, kvb_ref[...], preferred_element_type=F32)
        v_s[...] = kv[:, kw:kw + w].astype(BF16)
        kr = _apply_rope(mkv_ref[0, :, kv0 + MLA_KV_RANK:MLA_PAD].astype(F32), cos, sin, half)
        kr = pltpu.roll(kr, MLA_NOPE, 1)
        kp_s[...] = (kv[:, 0:kw] + _tile_lanes(kr, kw)).astype(BF16)
        lane = lax.broadcasted_iota(jnp.int32, (1, LANES), 1)
        rope_lane = (lane >= MLA_NOPE) & (lane < MLA_NOPE + MLA_ROPE)
        cos_s[...] = jnp.where(rope_lane, cos, 1.0)
        sin_s[...] = jnp.where(rope_lane, sin, 0.0)

    rows = pl.ds(pl.multiple_of(qi * tq, tq), tq)
    qa = _rms(mq_ref[0, :, 0:MLA_Q_RANK].astype(F32), qn_ref[...]).astype(BF16)
    q = jnp.dot(qa, qb_ref[...], preferred_element_type=F32)
    scale = (MLA_NOPE + MLA_ROPE) ** -0.5 * math.log2(math.e)
    q = (_apply_rope(q, cos_s[rows, :], sin_s[rows, :], half) * scale).astype(BF16)
    nt = (((1,), (1,)), ((), ()))
    outs = []
    for h in range(GROUP_HEADS):
        hl = slice(h * LANES, (h + 1) * LANES)
        s = lax.dot_general(q[:, hl], kp_s[:, hl], nt, preferred_element_type=F32)
        p = jnp.exp2(s - jnp.max(s, axis=-1, keepdims=True))
        den = jnp.sum(p, axis=-1, keepdims=True)
        outs.append(jnp.dot(p.astype(BF16), v_s[:, h * MLA_NOPE:(h + 1) * MLA_NOPE], preferred_element_type=F32) / den)
    o_ref[0] = jnp.concatenate(outs, axis=-1)


def _mla(ml, pos, qn_g, qb, kvn_g, kvb, tq=512):
    b, t, cw = ml.shape
    w = GROUP_WIDTH
    tq = min(tq, t)
    full = lambda a: pl.BlockSpec(a.shape, lambda i, j: (0, 0))
    qn_g = qn_g.reshape(1, -1)
    kvn_g = kvn_g.reshape(1, -1)
    return pl.pallas_call(
        functools.partial(_mla_kernel, tq=tq),
        grid=(b, t // tq),
        in_specs=[pl.BlockSpec((1, tq, cw), lambda i, j: (i, j, 0)),
                  pl.BlockSpec((1, t, cw), lambda i, j: (i, 0, 0)),
                  pl.BlockSpec((1, t, 1), lambda i, j: (i, 0, 0)),
                  full(qn_g), full(qb), full(kvn_g), full(kvb)],
        out_specs=pl.BlockSpec((1, tq, w), lambda i, j: (i, j, 0)),
        out_shape=jax.ShapeDtypeStruct((b, t, w), F32),
        scratch_shapes=[pltpu.VMEM((t, GROUP_HEADS * LANES), BF16), pltpu.VMEM((t, w), BF16),
                        pltpu.VMEM((t, LANES), F32), pltpu.VMEM((t, LANES), F32)],
        compiler_params=_cparams("parallel", "arbitrary"),
        name="mla",
    )(ml, ml, pos, qn_g, qb, kvn_g, kvb)


def _mla_weights(q_b, kv_b):
    qh = q_b.reshape(MLA_Q_RANK, GROUP_HEADS, MLA_NOPE + MLA_ROPE)
    qb = jnp.pad(qh, ((0, 0), (0, 0), (0, LANES - MLA_NOPE - MLA_ROPE))).reshape(MLA_Q_RANK, GROUP_HEADS * LANES)
    kvh = kv_b.reshape(MLA_KV_RANK, GROUP_HEADS, 2 * MLA_NOPE)
    kn = jnp.pad(kvh[:, :, :MLA_NOPE], ((0, 0), (0, 0), (0, LANES - MLA_NOPE))).reshape(MLA_KV_RANK, GROUP_HEADS * LANES)
    v = kvh[:, :, MLA_NOPE:].reshape(MLA_KV_RANK, GROUP_HEADS * MLA_NOPE)
    return qb.astype(BF16), jnp.concatenate([kn, v], axis=1).astype(BF16)


def _outproj_kernel(x_ref, yc_ref, yf_ref, yb_ref, g_ref, bonus_ref, lg_ref, lb_ref, yr_ref, ym_ref, w_ref, o_ref,
                    yw_s):
    w = GROUP_WIDTH
    _rwkv_finish(yf_ref, yb_ref, g_ref, bonus_ref, lg_ref, lb_ref, yw_s)
    acc = x_ref[...]
    for g, y in enumerate((yc_ref, yw_s, yr_ref, ym_ref)):
        acc = acc + jnp.dot(y[...].astype(BF16), w_ref[g * w:(g + 1) * w, :], preferred_element_type=F32)
    o_ref[...] = acc


def _outproj(x, y_conv, rwkv, y_ret, y_mla, w_out, tm=1024):
    yf, yb, g, bonus, lnx_g, lnx_b = rwkv
    n, d = x.shape
    w = GROUP_WIDTH
    b, nt, hd, _ = yf.shape
    tm = min(tm, nt * hd)
    per_seq = nt * hd // tm
    ytile = pl.BlockSpec((tm, w), lambda i: (i, 0))
    slabs = pl.BlockSpec((1, tm // hd, hd, w), lambda i: (i // per_seq, i % per_seq, 0, 0))
    vec = pl.BlockSpec((1, w), lambda i: (0, 0))
    return pl.pallas_call(
        _outproj_kernel,
        grid=(n // tm,),
        in_specs=[pl.BlockSpec((tm, d), lambda i: (i, 0)), ytile, slabs, slabs, ytile, ytile, vec, vec, ytile, ytile,
                  pl.BlockSpec(w_out.shape, lambda i: (0, 0))],
        out_specs=pl.BlockSpec((tm, d), lambda i: (i, 0)),
        out_shape=jax.ShapeDtypeStruct((n, d), F32),
        scratch_shapes=[pltpu.VMEM((tm, w), F32)],
        compiler_params=_cparams("parallel"),
        name="outproj",
    )(x, y_conv, yf, yb, g, bonus, lnx_g.reshape(1, w), lnx_b.reshape(1, w), y_ret, y_mla, w_out)


def kernel(x, positions, ffn1_norm, ffn1_w_gate, ffn1_w_up, ffn1_w_down, mix_norm, w_in, w_out, conv_w, rwkv_w0_f, rwkv_w0_b, rwkv_w2_f, rwkv_w2_b, rwkv_a0_f, rwkv_a0_b, rwkv_a2_f, rwkv_a2_b, rwkv_g2, rwkv_k_k, rwkv_k_a, rwkv_r_k, rwkv_lnx_g, rwkv_lnx_b, ret_gn_g, mla_q_a_norm, mla_q_b, mla_kv_a_norm, mla_kv_b, ffn2_norm, ffn2_w_gate, ffn2_w_up, ffn2_w_down, final_norm):
    b, t, d = x.shape
    n = b * t
    w = GROUP_WIDTH
    depth = w_in.shape[0]
    pos = positions.astype(F32).reshape(b, t, 1)
    xf = x.reshape(n, d)
    for l in range(depth):
        xf = _ffn(xf, ffn1_norm[l], ffn1_w_gate[l].astype(BF16), ffn1_w_up[l].astype(BF16),
                  ffn1_w_down[l].astype(BF16))
        wl = w_in[l].astype(BF16)
        w_mla = jnp.pad(wl[:, MLA_COLS[0]:MLA_COLS[1]], ((0, 0), (0, MLA_PAD - (MLA_COLS[1] - MLA_COLS[0]))))
        ws = [wl[:, CONV_COLS[0]:CONV_COLS[1]], wl[:, RWKV_COLS[0]:RWKV_COLS[1]], wl[:, RET_COLS[0]:RET_COLS[1]], w_mla]
        cv, rw, rt, ml = _inproj(xf, mix_norm[l], ws)
        y_conv = _conv(cv.reshape(b, t, -1), conv_w[l])
        rp = dict(w0_f=rwkv_w0_f[l], w0_b=rwkv_w0_b[l], w2_f=rwkv_w2_f[l], w2_b=rwkv_w2_b[l],
                  a0_f=rwkv_a0_f[l], a0_b=rwkv_a0_b[l], a2_f=rwkv_a2_f[l], a2_b=rwkv_a2_b[l],
                  g2=rwkv_g2[l], k_k=rwkv_k_k[l], k_a=rwkv_k_a[l], r_k=rwkv_r_k[l])
        pf, pb, g, bonus, vt = _rwkv_prep(rw, rp)
        yf, yb = _rwkv_scan(pf.reshape(b, t, -1), pb.reshape(b, t, -1), vt.reshape(b, t // SLAB, SLAB, w))
        y_ret = _ret(rt.reshape(b, t, -1), pos, ret_gn_g[l])
        qb, kvb = _mla_weights(mla_q_b[l], mla_kv_b[l])
        y_mla = _mla(ml.reshape(b, t, -1), pos, mla_q_a_norm[l], qb, mla_kv_a_norm[l], kvb)
        xf = _outproj(xf, y_conv.reshape(n, w), (yf, yb, g, bonus, rwkv_lnx_g[l], rwkv_lnx_b[l]),
                      y_ret.reshape(n, w), y_mla.reshape(n, w), w_out[l].astype(BF16))
        xf = _ffn(xf, ffn2_norm[l], ffn2_w_gate[l].astype(BF16), ffn2_w_up[l].astype(BF16),
                  ffn2_w_down[l].astype(BF16), final_g=final_norm if l == depth - 1 else None)
    return xf.reshape(b, t, d)
```
